```python
import jax, jax.numpy as jnp
from jax import lax
import numpy as np


D_MODEL = 4096
BATCH = 1
SEQ = 16384
DEPTH = 4
DEC_BATCH = 1
DEC_SEQ = 8192
PAST_LEN = 128

GRID_W = 64
HEAD_DIM = 128
N_Q_HEADS = 16
N_KV_HEADS = 4
Q_PER_KV = N_Q_HEADS // N_KV_HEADS
ATTN_WIDTH = N_Q_HEADS * HEAD_DIM
KV_WIDTH = N_KV_HEADS * HEAD_DIM
Q_BLOCK = 128
AXIS_DIM = HEAD_DIM // 2
ROPE_THETA = 10000.0
N_M_HEADS = 8
M_QK_DIM = 128
M_V_DIM = 256
M_QK_WIDTH = N_M_HEADS * M_QK_DIM
M_WIDTH = N_M_HEADS * M_V_DIM
M_CHUNK = 128
N_EXPERTS = 16
EXPERT_FF = 3072
CAPACITY_FACTOR = 2
N_BRANCH = 2
NORM_EPS = 1e-6
IN_SIZES = (ATTN_WIDTH, KV_WIDTH, KV_WIDTH, M_QK_WIDTH, M_QK_WIDTH, M_WIDTH, M_WIDTH,
            2 * N_M_HEADS, 2 * N_M_HEADS, N_BRANCH * D_MODEL)
N_IN = sum(IN_SIZES)
IN_SPLITS = tuple(int(s) for s in np.cumsum(IN_SIZES)[:-1])

kernel_name = 'hybrid_gqa_mlstm_ec_encoder'

F32 = jnp.float32


def rms_norm(x, g):
    xf = x.astype(F32)
    y = xf * lax.rsqrt(jnp.mean(xf * xf, axis=-1, keepdims=True) + NORM_EPS)
    return (y * g.astype(F32)).astype(x.dtype)


def axial_rope_tables(seq_len):
    rows = seq_len // GRID_W
    row_idx = jnp.repeat(jnp.arange(rows, dtype=F32), GRID_W)
    col_idx = jnp.tile(jnp.arange(GRID_W, dtype=F32), rows)
    inv_freq = ROPE_THETA ** (-jnp.arange(0, AXIS_DIM, 2, dtype=F32) / AXIS_DIM)
    ang_r = row_idx[:, None] * inv_freq[None, :]
    ang_c = col_idx[:, None] * inv_freq[None, :]
    return (jnp.cos(ang_r), jnp.sin(ang_r), jnp.cos(ang_c), jnp.sin(ang_c))


def apply_axial_rope(x, rope):
    cos_r, sin_r, cos_c, sin_c = rope

    def rot(xp, cos, sin):
        x1, x2 = jnp.split(xp, 2, axis=-1)
        c = cos[None, :, None, :]
        s = sin[None, :, None, :]
        return jnp.concatenate([x1 * c - x2 * s, x1 * s + x2 * c], axis=-1)

    xr, xc = jnp.split(x, 2, axis=-1)
    return jnp.concatenate([rot(xr, cos_r, sin_r), rot(xc, cos_c, sin_c)], axis=-1)


def attention_branch(q, k, v, rope, g_q, g_k):
    B, S, _ = q.shape
    dt = v.dtype
    q = q.reshape(B, S, N_Q_HEADS, HEAD_DIM)
    k = k.reshape(B, S, N_KV_HEADS, HEAD_DIM)
    v = v.reshape(B, S, N_KV_HEADS, HEAD_DIM)
    q = (apply_axial_rope(rms_norm(q, g_q).astype(F32), rope) * HEAD_DIM ** -0.5).astype(dt)
    k = apply_axial_rope(rms_norm(k, g_k).astype(F32), rope).astype(dt)
    n_blk = S // Q_BLOCK
    qb = q.reshape(B, n_blk, Q_BLOCK, N_KV_HEADS, Q_PER_KV, HEAD_DIM).transpose(1, 0, 2, 3, 4, 5)

    def block(qi):
        s = jnp.einsum('bqhgd,bkhd->bhgqk', qi, k, preferred_element_type=F32)
        p = jax.nn.softmax(s, axis=-1).astype(dt)
        return jnp.einsum('bhgqk,bkhd->bqhgd', p, v)

    o = lax.map(block, qb)
    return o.transpose(1, 0, 2, 3, 4, 5).reshape(B, S, ATTN_WIDTH)


def mlstm_direction(q, k, v, ig, lf):
    B, S, H, DK = q.shape
    DV = v.shape[-1]
    nc = S // M_CHUNK

    def chunks(a):
        a = a.reshape((B, nc, M_CHUNK, H) + a.shape[3:])
        return jnp.moveaxis(jnp.moveaxis(a, 1, 0), 3, 2)

    lower = jnp.tril(jnp.ones((M_CHUNK, M_CHUNK), dtype=bool))

    def step(carry, xs):
        C, n, m = carry
        qc, kc, vc, ic, fc = xs
        b = jnp.cumsum(fc, axis=-1)
        dmat = jnp.where(lower, b[..., :, None] - b[..., None, :] + ic[..., None, :], -jnp.inf)
        inter = b + m[..., None]
        m_t = jnp.maximum(inter, jnp.max(dmat, axis=-1))
        w = jnp.exp(dmat - m_t[..., None]) * jnp.einsum('bhtd,bhsd->bhts', qc, kc)
        a_int = jnp.exp(inter - m_t)
        num = jnp.einsum('bhts,bhsv->bhtv', w, vc) + a_int[..., None] * jnp.einsum('bhvd,bhtd->bhtv', C, qc)
        den = jnp.sum(w, axis=-1) + a_int * jnp.einsum('bhd,bhtd->bht', n, qc)
        h = num / jnp.maximum(jnp.abs(den), jnp.exp(-m_t))[..., None]
        b_last = b[..., -1]
        g = b_last[..., None] - b + ic
        m_new = jnp.maximum(b_last + m, jnp.max(g, axis=-1))
        decay = jnp.exp(b_last + m - m_new)
        wk = jnp.exp(g - m_new[..., None])
        C_new = decay[..., None, None] * C + jnp.einsum('bhs,bhsv,bhsd->bhvd', wk, vc, kc)
        n_new = decay[..., None] * n + jnp.einsum('bhs,bhsd->bhd', wk, kc)
        return (C_new, n_new, m_new), h

    init = (jnp.zeros((B, H, DV, DK), F32), jnp.zeros((B, H, DK), F32), jnp.zeros((B, H), F32))
    _, h = lax.scan(step, init, (chunks(q), chunks(k), chunks(v), chunks(ig), chunks(lf)))
    return jnp.swapaxes(jnp.moveaxis(h, 0, 1), 2, 3).reshape(B, S, H, DV)


def mlstm_branch(q, k, v, o, ig, fg, b_i, b_f, g_mh):
    B, S, _ = q.shape
    q = q.reshape(B, S, N_M_HEADS, M_QK_DIM).astype(F32)
    k = k.reshape(B, S, N_M_HEADS, M_QK_DIM).astype(F32) * M_QK_DIM ** -0.5
    v = v.reshape(B, S, N_M_HEADS, M_V_DIM).astype(F32)
    ig = ig.reshape(B, S, 2, N_M_HEADS).astype(F32) + b_i.astype(F32)
    lf = jax.nn.log_sigmoid(fg.reshape(B, S, 2, N_M_HEADS).astype(F32) + b_f.astype(F32))
    h_fwd = mlstm_direction(q, k, v, ig[:, :, 0], lf[:, :, 0])
    rev = lambda a: jnp.flip(a, axis=1)
    h_bwd = rev(mlstm_direction(rev(q), rev(k), rev(v), rev(ig[:, :, 1]), rev(lf[:, :, 1])))
    h = h_fwd + h_bwd
    h = h * lax.rsqrt(jnp.mean(h * h, axis=-1, keepdims=True) + NORM_EPS)
    h = h.reshape(B, S, M_WIDTH) * g_mh.astype(F32) * jax.nn.sigmoid(o.astype(F32))
    return h.astype(o.dtype)


def expert_choice_ffn(xn, w_router, w_gate, w_up, w_down):
    B, S, D = xn.shape
    n_tok = B * S
    cap = CAPACITY_FACTOR * n_tok // N_EXPERTS
    xf = xn.reshape(n_tok, D)
    aff = jax.nn.softmax((xf @ w_router).astype(F32), axis=-1)
    vals, idx = lax.top_k(aff.T, cap)
    xs = xf[idx]
    h = jax.nn.silu(jnp.einsum('ecd,edf->ecf', xs, w_gate)) * jnp.einsum('ecd,edf->ecf', xs, w_up)
    y = jnp.einsum('ecf,efd->ecd', h, w_down) * vals[..., None].astype(xn.dtype)
    out = jnp.zeros_like(xf).at[idx.reshape(-1)].add(y.reshape(-1, D))
    return out.reshape(B, S, D)


def encoder_layer(x, rope, g_mix, w_in, g_q, g_k, b_i, b_f, g_mh, w_br_attn, w_br_mlstm, w_out,
                  g_ffn, w_router, w_e_gate, w_e_up, w_e_down):
    B, S, D = x.shape
    xn = rms_norm(x, g_mix)
    u = xn @ w_in
    qa, ka, va, qm, km, vm, om, igm, fgm, gates = jnp.split(u, IN_SPLITS, axis=-1)
    ya = attention_branch(qa, ka, va, rope, g_q, g_k) @ w_br_attn
    ym = mlstm_branch(qm, km, vm, om, igm, fgm, b_i, b_f, g_mh) @ w_br_mlstm
    g = jax.nn.sigmoid(gates.astype(F32)).reshape(B, S, N_BRANCH, D)
    merged = (g[:, :, 0] * ya.astype(F32) + g[:, :, 1] * ym.astype(F32)).astype(x.dtype)
    x = x + merged @ w_out
    x = x + expert_choice_ffn(rms_norm(x, g_ffn), w_router, w_e_gate, w_e_up, w_e_down)
    return x


def run_trunk(x, g_mix, w_in, g_q, g_k, b_igate, b_fgate, g_mhead, w_br_attn, w_br_mlstm, w_out,
              g_ffn, w_router, w_e_gate, w_e_up, w_e_down, g_final):
    rope = axial_rope_tables(x.shape[1])
    for l in range(DEPTH):
        x = encoder_layer(x, rope, g_mix[l], w_in[l], g_q[l], g_k[l], b_igate[l], b_fgate[l], g_mhead[l],
                          w_br_attn[l], w_br_mlstm[l], w_out[l], g_ffn[l], w_router[l],
                          w_e_gate[l], w_e_up[l], w_e_down[l])
    return rms_norm(x, g_final)


def setup_inputs(seed: int = 0) -> dict:
    key = jax.random.key(seed)
    ks = jax.random.split(key, 18)
    nrm = jax.random.normal
    return {
        'x_prompt': nrm(ks[0], (BATCH, SEQ, D_MODEL), F32),
        'x_sample': nrm(ks[1], (DEC_BATCH, DEC_SEQ, D_MODEL), F32),
        'g_mix': 1.0 + 0.02 * nrm(ks[2], (DEPTH, D_MODEL), F32),
        'w_in': nrm(ks[3], (DEPTH, D_MODEL, N_IN), F32) * D_MODEL ** -0.5,
        'g_q': 1.0 + 0.02 * nrm(ks[4], (DEPTH, HEAD_DIM), F32),
        'g_k': 1.0 + 0.02 * nrm(ks[5], (DEPTH, HEAD_DIM), F32),
        'b_igate': 0.1 * nrm(ks[6], (DEPTH, 2, N_M_HEADS), F32),
        'b_fgate': 3.0 + 0.5 * nrm(ks[7], (DEPTH, 2, N_M_HEADS), F32),
        'g_mhead': 1.0 + 0.02 * nrm(ks[8], (DEPTH, M_WIDTH), F32),
        'w_br_attn': nrm(ks[9], (DEPTH, ATTN_WIDTH, D_MODEL), F32) * ATTN_WIDTH ** -0.5,
        'w_br_mlstm': nrm(ks[10], (DEPTH, M_WIDTH, D_MODEL), F32) * M_WIDTH ** -0.5,
        'w_out': nrm(ks[11], (DEPTH, D_MODEL, D_MODEL), F32) * D_MODEL ** -0.5,
        'g_ffn': 1.0 + 0.02 * nrm(ks[12], (DEPTH, D_MODEL), F32),
        'w_router': nrm(ks[13], (DEPTH, D_MODEL, N_EXPERTS), F32) * D_MODEL ** -0.5,
        'w_e_gate': nrm(ks[14], (DEPTH, N_EXPERTS, D_MODEL, EXPERT_FF), F32) * D_MODEL ** -0.5,
        'w_e_up': nrm(ks[15], (DEPTH, N_EXPERTS, D_MODEL, EXPERT_FF), F32) * D_MODEL ** -0.5,
        'w_e_down': nrm(ks[16], (DEPTH, N_EXPERTS, EXPERT_FF, D_MODEL), F32) * EXPERT_FF ** -0.5,
        'g_final': 1.0 + 0.02 * nrm(ks[17], (D_MODEL,), F32),
    }


def reference(x_prompt, x_sample, g_mix, w_in, g_q, g_k, b_igate, b_fgate, g_mhead, w_br_attn, w_br_mlstm,
              w_out, g_ffn, w_router, w_e_gate, w_e_up, w_e_down, g_final):
    y_prompt = run_trunk(x_prompt, g_mix, w_in, g_q, g_k, b_igate, b_fgate, g_mhead, w_br_attn, w_br_mlstm,
                         w_out, g_ffn, w_router, w_e_gate, w_e_up, w_e_down, g_final)
    y_sample = run_trunk(x_sample, g_mix, w_in, g_q, g_k, b_igate, b_fgate, g_mhead, w_br_attn, w_br_mlstm,
                         w_out, g_ffn, w_router, w_e_gate, w_e_up, w_e_down, g_final)
    return (y_prompt, y_sample)
```

```python
import functools

import jax
import jax.numpy as jnp
import numpy as np
from jax import lax
from jax.experimental import pallas as pl
from jax.experimental.pallas import tpu as pltpu

F32 = jnp.float32
BF16 = jnp.bfloat16
I32 = jnp.int32

GRID_W = 64
HEAD_DIM = 128
N_Q_HEADS = 16
N_KV_HEADS = 4
Q_PER_KV = N_Q_HEADS // N_KV_HEADS
ATTN_WIDTH = N_Q_HEADS * HEAD_DIM
KV_WIDTH = N_KV_HEADS * HEAD_DIM
ROPE_THETA = 10000.0
N_M_HEADS = 8
M_QK_DIM = 128
M_V_DIM = 256
M_QK_WIDTH = N_M_HEADS * M_QK_DIM
M_WIDTH = N_M_HEADS * M_V_DIM
N_EXPERTS = 16
CAPACITY_FACTOR = 2
NORM_EPS = 1e-6

OFF_Q = 0
OFF_K = OFF_Q + ATTN_WIDTH
OFF_V = OFF_K + KV_WIDTH
OFF_MQ = OFF_V + KV_WIDTH
OFF_MK = OFF_MQ + M_QK_WIDTH
OFF_MV = OFF_MK + M_QK_WIDTH
OFF_MO = OFF_MV + M_WIDTH
OFF_IF = OFF_MO + M_WIDTH
N_IF = 4 * N_M_HEADS
OFF_G = OFF_IF

LANES = 128
VMEM_LIMIT = 56 * 1024 * 1024

M_CHUNK = 128


def _cparams(sem, vmem=VMEM_LIMIT):
    return pltpu.CompilerParams(dimension_semantics=sem, vmem_limit_bytes=vmem)


def _rmsnorm_kernel(x_ref, g_ref, o_ref):
    x = x_ref[...].astype(F32)
    ms = jnp.mean(x * x, axis=-1, keepdims=True)
    o_ref[...] = (x * lax.rsqrt(ms + NORM_EPS) * g_ref[...]).astype(o_ref.dtype)


def rmsnorm(x, g, out_dtype, tm=512):
    m, d = x.shape
    return pl.pallas_call(
        _rmsnorm_kernel,
        grid=(m // tm,),
        in_specs=[pl.BlockSpec((tm, d), lambda i: (i, 0)), pl.BlockSpec((1, d), lambda i: (0, 0))],
        out_specs=pl.BlockSpec((tm, d), lambda i: (i, 0)),
        out_shape=jax.ShapeDtypeStruct((m, d), out_dtype),
        compiler_params=_cparams(("parallel",)),
        name="rmsnorm",
    )(x, g.reshape(1, d).astype(F32))


def _mm_kernel(a_ref, b_ref, o_ref):
    o_ref[...] = jnp.dot(a_ref[...], b_ref[...], preferred_element_type=F32).astype(o_ref.dtype)


def matmul(a, b, out_dtype, tm=1024, tn=1024):
    m, k = a.shape
    _, n = b.shape
    tm, tn = min(tm, m), min(tn, n)
    return pl.pallas_call(
        _mm_kernel,
        grid=(m // tm, n // tn),
        in_specs=[pl.BlockSpec((tm, k), lambda i, j: (i, 0)), pl.BlockSpec((k, tn), lambda i, j: (0, j))],
        out_specs=pl.BlockSpec((tm, tn), lambda i, j: (i, j)),
        out_shape=jax.ShapeDtypeStruct((m, n), out_dtype),
        compiler_params=_cparams(("parallel", "parallel")),
        name="matmul",
    )(a, b)


def _qkprep_kernel(u_ref, g_ref, c_ref, s_ref, o_ref):
    n_heads = o_ref.shape[1] // HEAD_DIM
    cos = c_ref[...]
    sin = s_ref[...]
    lane = lax.broadcasted_iota(I32, cos.shape, 1)
    first_half = (lane % (HEAD_DIM // 2)) < (HEAD_DIM // 4)
    for h in range(n_heads):
        sl = slice(h * HEAD_DIM, (h + 1) * HEAD_DIM)
        x = u_ref[:, sl].astype(F32)
        ms = jnp.mean(x * x, axis=-1, keepdims=True)
        xn = x * lax.rsqrt(ms + NORM_EPS) * g_ref[h:h + 1, :]
        partner = jnp.where(first_half, pltpu.roll(xn, HEAD_DIM - HEAD_DIM // 4, 1), pltpu.roll(xn, HEAD_DIM // 4, 1))
        o_ref[:, sl] = (xn * cos + partner * sin).astype(o_ref.dtype)


def qk_prep(u, g_tab, cos_tab, sin_tab, tm=512):
    m = u.shape[0]
    w = ATTN_WIDTH + KV_WIDTH
    return pl.pallas_call(
        _qkprep_kernel,
        grid=(m // tm,),
        in_specs=[
            pl.BlockSpec((tm, w), lambda i: (i, 0)),
            pl.BlockSpec((w // HEAD_DIM, HEAD_DIM), lambda i: (0, 0)),
            pl.BlockSpec((tm, HEAD_DIM), lambda i: (i, 0)),
            pl.BlockSpec((tm, HEAD_DIM), lambda i: (i, 0)),
        ],
        out_specs=pl.BlockSpec((tm, w), lambda i: (i, 0)),
        out_shape=jax.ShapeDtypeStruct((m, w), BF16),
        compiler_params=_cparams(("parallel",)),
        name="qk_prep",
    )(u, g_tab, cos_tab, sin_tab)


def _flash_kernel(q_ref, k_ref, v_ref, o_ref, qs_ref, m_ref, l_ref, acc_ref):
    ki = pl.program_id(2)
    tq = q_ref.shape[0]

    @pl.when(ki == 0)
    def _():
        for g in range(Q_PER_KV):
            qs_ref[g * tq:(g + 1) * tq, :] = q_ref[:, g * HEAD_DIM:(g + 1) * HEAD_DIM]
        m_ref[...] = jnp.full(m_ref.shape, -jnp.inf, F32)
        l_ref[...] = jnp.zeros(l_ref.shape, F32)
        acc_ref[...] = jnp.zeros(acc_ref.shape, F32)

    s = lax.dot_general(qs_ref[...], k_ref[...], (((1,), (1,)), ((), ())), preferred_element_type=F32)
    m_prev = m_ref[...]
    m_new = jnp.maximum(m_prev, jnp.max(s, axis=-1, keepdims=True))
    alpha = jnp.exp(m_prev - m_new)
    p = jnp.exp(s - m_new)
    l_ref[...] = alpha * l_ref[...] + jnp.sum(p, axis=-1, keepdims=True)
    acc_ref[...] = alpha * acc_ref[...] + jnp.dot(p.astype(BF16), v_ref[...], preferred_element_type=F32)
    m_ref[...] = m_new

    @pl.when(ki == pl.num_programs(2) - 1)
    def _():
        o = acc_ref[...] / l_ref[...]
        for g in range(Q_PER_KV):
            o_ref[:, g * HEAD_DIM:(g + 1) * HEAD_DIM] = o[g * tq:(g + 1) * tq, :].astype(o_ref.dtype)


def flash_attention(qk, u, tq=256, tk=1024):
    s_len = qk.shape[0]
    tq, tk = min(tq, s_len), min(tk, s_len)
    gw = Q_PER_KV * HEAD_DIM
    rows = Q_PER_KV * tq
    return pl.pallas_call(
        _flash_kernel,
        grid=(N_KV_HEADS, s_len // tq, s_len // tk),
        in_specs=[
            pl.BlockSpec((tq, gw), lambda h, i, j: (i, h)),
            pl.BlockSpec((tk, HEAD_DIM), lambda h, i, j: (j, ATTN_WIDTH // HEAD_DIM + h)),
            pl.BlockSpec((tk, HEAD_DIM), lambda h, i, j: (j, OFF_V // HEAD_DIM + h)),
        ],
        out_specs=pl.BlockSpec((tq, gw), lambda h, i, j: (i, h)),
        out_shape=jax.ShapeDtypeStruct((s_len, ATTN_WIDTH), BF16),
        scratch_shapes=[
            pltpu.VMEM((rows, HEAD_DIM), BF16),
            pltpu.VMEM((rows, 1), F32),
            pltpu.VMEM((rows, 1), F32),
            pltpu.VMEM((rows, HEAD_DIM), F32),
        ],
        compiler_params=_cparams(("parallel", "parallel", "arbitrary")),
        name="flash_attention",
    )(qk, qk, u)


def _log_sigmoid(x):
    return jnp.minimum(x, 0.0) - jnp.log1p(jnp.exp(-jnp.abs(x)))


def _mlstm_kernel(q_ref, k_ref, v_ref, gate_ref, bias_ref, o_ref, ct_ref, n_ref, m_ref):
    d = pl.program_id(0)
    h = pl.program_id(1)
    c = pl.program_id(2)
    L = q_ref.shape[0]

    @pl.when(c == 0)
    def _():
        ct_ref[...] = jnp.zeros(ct_ref.shape, F32)
        n_ref[...] = jnp.zeros(n_ref.shape, F32)
        m_ref[...] = jnp.zeros(m_ref.shape, F32)

    row = d * N_M_HEADS + h
    i_row = gate_ref[pl.ds(row, 1), :] + bias_ref[pl.ds(row, 1), :]
    f_row = _log_sigmoid(gate_ref[pl.ds(2 * N_M_HEADS + row, 1), :] + bias_ref[pl.ds(2 * N_M_HEADS + row, 1), :])

    t_idx = lax.broadcasted_iota(I32, (L, L), 0)
    s_idx = lax.broadcasted_iota(I32, (L, L), 1)
    ahead = (s_idx - t_idx) * (1 - 2 * d)
    valid = ahead <= 0
    csum = jnp.where(ahead >= 0, 1.0, 0.0).astype(F32)
    b_row = jnp.dot(jnp.broadcast_to(f_row, (8, L)), csum, preferred_element_type=F32,
                    precision=lax.Precision.HIGHEST)[0:1, :]
    total = jnp.sum(f_row, axis=-1, keepdims=True)
    b_col = jnp.transpose(jnp.broadcast_to(b_row, (L, L)))
    i_col = jnp.transpose(jnp.broadcast_to(i_row, (L, L)))

    m_prev = m_ref[:, 0:1]
    dmat = jnp.where(valid, b_col - b_row + i_row, -jnp.inf)
    inter = b_col[:, 0:1] + m_prev
    m_t = jnp.maximum(inter, jnp.max(dmat, axis=-1, keepdims=True))
    dexp = jnp.exp(dmat - m_t)

    qb = q_ref[...]
    kb = (k_ref[...].astype(F32) * (M_QK_DIM ** -0.5)).astype(BF16)
    vb = v_ref[...]
    qk = lax.dot_general(qb, kb, (((1,), (1,)), ((), ())), preferred_element_type=F32)
    w = dexp * qk
    a_int = jnp.exp(inter - m_t)
    ct = ct_ref[...]
    num = (jnp.dot(w.astype(BF16), vb, preferred_element_type=F32)
           + a_int * jnp.dot(qb, ct.astype(BF16), preferred_element_type=F32))
    n_row = n_ref[...]
    den = jnp.sum(w, axis=-1, keepdims=True) + a_int * jnp.sum(qb.astype(F32) * n_row, axis=-1, keepdims=True)
    o_ref[...] = (num / jnp.maximum(jnp.abs(den), jnp.exp(-m_t))).astype(o_ref.dtype)

    g_col = total - b_col[:, 0:1] + i_col[:, 0:1]
    m_new = jnp.maximum(total + m_prev, jnp.max(g_col, axis=0, keepdims=True))
    decay = jnp.exp(total + m_prev - m_new)
    kw = kb.astype(F32) * jnp.exp(g_col - m_new)
    ct_ref[...] = decay * ct + lax.dot_general(kw.astype(BF16), vb, (((0,), (0,)), ((), ())),
                                               preferred_element_type=F32)
    n_ref[...] = decay * n_row + jnp.sum(kw, axis=0, keepdims=True)
    m_ref[...] = jnp.broadcast_to(m_new, m_ref.shape)


def mlstm_scan(u, gates_t, bias_tab):
    s_len = u.shape[0]
    L = M_CHUNK
    nc = s_len // L

    def cidx(d, c):
        return c + d * (nc - 1 - 2 * c)

    return pl.pallas_call(
        _mlstm_kernel,
        grid=(2, N_M_HEADS, nc),
        in_specs=[
            pl.BlockSpec((L, M_QK_DIM), lambda d, h, c: (cidx(d, c), OFF_MQ // M_QK_DIM + h)),
            pl.BlockSpec((L, M_QK_DIM), lambda d, h, c: (cidx(d, c), OFF_MK // M_QK_DIM + h)),
            pl.BlockSpec((L, M_V_DIM), lambda d, h, c: (cidx(d, c), OFF_MV // M_V_DIM + h)),
            pl.BlockSpec((N_IF, L), lambda d, h, c: (0, cidx(d, c))),
            pl.BlockSpec((N_IF, LANES), lambda d, h, c: (0, 0)),
        ],
        out_specs=pl.BlockSpec((None, L, M_V_DIM), lambda d, h, c: (d, cidx(d, c), h)),
        out_shape=jax.ShapeDtypeStruct((2, s_len, M_WIDTH), F32),
        scratch_shapes=[
            pltpu.VMEM((M_QK_DIM, M_V_DIM), F32),
            pltpu.VMEM((1, M_QK_DIM), F32),
            pltpu.VMEM((1, LANES), F32),
        ],
        compiler_params=_cparams(("parallel", "parallel", "arbitrary")),
        name="mlstm_scan",
    )(u, u, u, gates_t, bias_tab)


def _mlstm_post_kernel(h_ref, o_ref, g_ref, out_ref):
    hsum = h_ref[0] + h_ref[1]
    hn = hsum * lax.rsqrt(jnp.mean(hsum * hsum, axis=-1, keepdims=True) + NORM_EPS)
    out_ref[...] = (hn * g_ref[...] * jax.nn.sigmoid(o_ref[...].astype(F32))).astype(out_ref.dtype)


def mlstm_post(hd, u, g_mh, tm=512):
    s_len = u.shape[0]
    tm = min(tm, s_len)
    return pl.pallas_call(
        _mlstm_post_kernel,
        grid=(s_len // tm, N_M_HEADS),
        in_specs=[
            pl.BlockSpec((2, tm, M_V_DIM), lambda i, h: (0, i, h)),
            pl.BlockSpec((tm, M_V_DIM), lambda i, h: (i, OFF_MO // M_V_DIM + h)),
            pl.BlockSpec((1, M_V_DIM), lambda i, h: (0, h)),
        ],
        out_specs=pl.BlockSpec((tm, M_V_DIM), lambda i, h: (i, h)),
        out_shape=jax.ShapeDtypeStruct((s_len, M_WIDTH), BF16),
        compiler_params=_cparams(("parallel", "parallel")),
        name="mlstm_post",
    )(hd, u, g_mh.reshape(1, M_WIDTH).astype(F32))


def _merge_kernel(a_ref, m_ref, wa_ref, wm_ref, ga_ref, gm_ref, o_ref):
    ya = jnp.dot(a_ref[...], wa_ref[...], preferred_element_type=F32)
    ym = jnp.dot(m_ref[...], wm_ref[...], preferred_element_type=F32)
    ga = jax.nn.sigmoid(ga_ref[...].astype(F32))
    gm = jax.nn.sigmoid(gm_ref[...].astype(F32))
    o_ref[...] = (ga * ya + gm * ym).astype(o_ref.dtype)


def branch_merge(attn, mh, w_a, w_m, u, tm=1024, tn=1024):
    m, ka = attn.shape
    km = mh.shape[1]
    d = w_a.shape[1]
    tm = min(tm, m)
    gblk = OFF_G // tn
    return pl.pallas_call(
        _merge_kernel,
        grid=(m // tm, d // tn),
        in_specs=[
            pl.BlockSpec((tm, ka), lambda i, j: (i, 0)),
            pl.BlockSpec((tm, km), lambda i, j: (i, 0)),
            pl.BlockSpec((ka, tn), lambda i, j: (0, j)),
            pl.BlockSpec((km, tn), lambda i, j: (0, j)),
            pl.BlockSpec((tm, tn), lambda i, j: (i, gblk + j)),
            pl.BlockSpec((tm, tn), lambda i, j: (i, gblk + d // tn + j)),
        ],
        out_specs=pl.BlockSpec((tm, tn), lambda i, j: (i, j)),
        out_shape=jax.ShapeDtypeStruct((m, d), BF16),
        compiler_params=_cparams(("parallel", "parallel")),
        name="branch_merge",
    )(attn, mh, w_a, w_m, u, u)


def _resid_mm_kernel(a_ref, w_ref, x_ref, o_ref):
    o_ref[...] = x_ref[...] + jnp.dot(a_ref[...], w_ref[...], preferred_element_type=F32)


def residual_matmul(a, w, x, tm=1024, tn=1024):
    m, k = a.shape
    n = w.shape[1]
    tm = min(tm, m)
    return pl.pallas_call(
        _resid_mm_kernel,
        grid=(m // tm, n // tn),
        in_specs=[
            pl.BlockSpec((tm, k), lambda i, j: (i, 0)),
            pl.BlockSpec((k, tn), lambda i, j: (0, j)),
            pl.BlockSpec((tm, tn), lambda i, j: (i, j)),
        ],
        out_specs=pl.BlockSpec((tm, tn), lambda i, j: (i, j)),
        out_shape=jax.ShapeDtypeStruct((m, n), F32),
        input_output_aliases={2: 0},
        compiler_params=_cparams(("parallel", "parallel")),
        name="residual_matmul",
    )(a, w, x)


def _router_kernel(x_ref, g_ref, w_ref, o_ref):
    x = x_ref[...]
    ms = jnp.mean(x * x, axis=-1, keepdims=True)
    xn = (x * lax.rsqrt(ms + NORM_EPS) * g_ref[...]).astype(BF16)
    logits = jnp.dot(xn, w_ref[...], preferred_element_type=F32)
    lane = lax.broadcasted_iota(I32, logits.shape, 1)
    logits = jnp.where(lane < N_EXPERTS, logits, -jnp.inf)
    e = jnp.exp(logits - jnp.max(logits, axis=-1, keepdims=True))
    o_ref[...] = e / jnp.sum(e, axis=-1, keepdims=True)


def router(x, g, w_pad, tm=512):
    m, d = x.shape
    tm = min(tm, m)
    return pl.pallas_call(
        _router_kernel,
        grid=(m // tm,),
        in_specs=[
            pl.BlockSpec((tm, d), lambda i: (i, 0)),
            pl.BlockSpec((1, d), lambda i: (0, 0)),
            pl.BlockSpec((d, LANES), lambda i: (0, 0)),
        ],
        out_specs=pl.BlockSpec((tm, LANES), lambda i: (i, 0)),
        out_shape=jax.ShapeDtypeStruct((m, LANES), F32),
        compiler_params=_cparams(("parallel",)),
        name="router",
    )(x, g.reshape(1, d).astype(F32), w_pad)


def _select_kernel(aff_ref, idx_ref, val_ref, *, cap):
    a = aff_ref[0]
    nb = a.shape[0]
    bits = pltpu.bitcast(a, I32)
    capf = jnp.float32(cap)

    def count(mask):
        return jnp.sum(jnp.sum(mask.astype(F32), axis=1, keepdims=True), axis=0, keepdims=True)

    def body(i, t):
        cand = t | jnp.left_shift(jnp.int32(1), jnp.int32(30) - i)
        return jnp.where(count(bits >= cand) >= capf, cand, t)

    thr = lax.fori_loop(0, 31, body, jnp.zeros((1, 1), I32))
    gt = bits > thr
    eq = bits == thr
    need = capf - count(gt)

    k_i = lax.broadcasted_iota(I32, (LANES, LANES), 0)
    s_i = lax.broadcasted_iota(I32, (LANES, LANES), 1)
    incl = (k_i <= s_i).astype(BF16)
    rb = lax.broadcasted_iota(I32, (nb, nb), 0)
    cb = lax.broadcasted_iota(I32, (nb, nb), 1)
    blk_strict = (cb < rb).astype(BF16)
    blk_incl = (rb <= cb).astype(BF16)

    eq_b = eq.astype(BF16)
    eq_in = jnp.dot(eq_b, incl, preferred_element_type=F32)
    eq_tot = jnp.broadcast_to(eq_in[:, LANES - 1:LANES], (nb, LANES)).astype(BF16)
    eq_off = jnp.dot(blk_strict, eq_tot, preferred_element_type=F32)
    eq_before = eq_off + eq_in - eq.astype(F32)
    sel = gt | (eq & (eq_before < need))

    sel_b = sel.astype(BF16)
    cin = jnp.dot(sel_b, incl, preferred_element_type=F32)
    tot_row = lax.dot_general(jnp.ones((8, LANES), BF16), sel_b, (((1,), (1,)), ((), ())),
                              preferred_element_type=F32)
    binc_row = jnp.dot(tot_row.astype(BF16), blk_incl, preferred_element_type=F32)[0:1, :]
    bexc_row = binc_row - tot_row[0:1, :]

    j_col = lax.broadcasted_iota(I32, (cap, 1), 0).astype(F32)
    kj = jnp.sum((binc_row <= j_col).astype(F32), axis=1, keepdims=True)
    blk_lane = lax.broadcasted_iota(I32, (cap, nb), 1).astype(F32)
    onehot = blk_lane == kj
    rows_c = jnp.dot(onehot.astype(BF16), cin.astype(BF16), preferred_element_type=F32)
    bex_k = jnp.sum(jnp.where(onehot, bexc_row, 0.0), axis=1, keepdims=True)
    r = j_col - bex_k
    pos = jnp.sum((rows_c <= r).astype(F32), axis=1, keepdims=True)
    idx_ref[0] = (kj * LANES + pos).astype(I32)
    rows_a = jnp.dot(onehot.astype(F32), a, preferred_element_type=F32, precision=lax.Precision.HIGHEST)
    pos_lane = lax.broadcasted_iota(I32, (cap, LANES), 1).astype(F32)
    val_ref[0] = jnp.sum(jnp.where(pos_lane == pos, rows_a, 0.0), axis=1, keepdims=True)


def expert_select(aff3, cap):
    ne, nb, _ = aff3.shape
    return pl.pallas_call(
        functools.partial(_select_kernel, cap=cap),
        grid=(ne,),
        in_specs=[pl.BlockSpec((1, nb, LANES), lambda e: (e, 0, 0))],
        out_specs=[pl.BlockSpec((1, cap, 1), lambda e: (e, 0, 0)), pl.BlockSpec((1, cap, 1), lambda e: (e, 0, 0))],
        out_shape=[jax.ShapeDtypeStruct((ne, cap, 1), I32), jax.ShapeDtypeStruct((ne, cap, 1), F32)],
        compiler_params=_cparams(("parallel",)),
        name="expert_select",
    )(aff3)


def _row_copy(idx_ref, slot, r, hbm_ref, vmem_ref, sem, to_vmem):
    hbm = hbm_ref.at[pl.ds(idx_ref[slot], 1), :]
    vm = vmem_ref.at[pl.ds(r, 1), :]
    return pltpu.make_async_copy(hbm, vm, sem) if to_vmem else pltpu.make_async_copy(vm, hbm, sem)


def _move_rows(idx_ref, base, rows, hbm_ref, vmem_ref, sem, to_vmem):
    def start(r, carry):
        _row_copy(idx_ref, base + r, r, hbm_ref, vmem_ref, sem, to_vmem).start()
        return carry

    def wait(r, carry):
        _row_copy(idx_ref, base + r, r, hbm_ref, vmem_ref, sem, to_vmem).wait()
        return carry

    lax.fori_loop(0, rows, start, 0)
    lax.fori_loop(0, rows, wait, 0)


def _gather_norm_kernel(idx_ref, x_ref, g_ref, o_ref, buf_ref, sem):
    rows = buf_ref.shape[0]
    _move_rows(idx_ref, pl.program_id(0) * rows, rows, x_ref, buf_ref, sem, True)
    x = buf_ref[...]
    ms = jnp.mean(x * x, axis=-1, keepdims=True)
    o_ref[...] = (x * lax.rsqrt(ms + NORM_EPS) * g_ref[...]).astype(o_ref.dtype)


def gather_norm(idx_flat, x, g, rows=512):
    n_slots = idx_flat.shape[0]
    d = x.shape[1]
    rows = min(rows, n_slots)
    grid_spec = pltpu.PrefetchScalarGridSpec(
        num_scalar_prefetch=1,
        grid=(n_slots // rows,),
        in_specs=[pl.BlockSpec(memory_space=pl.ANY), pl.BlockSpec((1, d), lambda i, idx: (0, 0))],
        out_specs=pl.BlockSpec((rows, d), lambda i, idx: (i, 0)),
        scratch_shapes=[pltpu.VMEM((rows, d), F32), pltpu.SemaphoreType.DMA(())],
    )
    return pl.pallas_call(
        _gather_norm_kernel,
        grid_spec=grid_spec,
        out_shape=jax.ShapeDtypeStruct((n_slots, d), BF16),
        compiler_params=_cparams(("arbitrary",)),
        name="gather_norm",
    )(idx_flat, x, g.reshape(1, d).astype(F32))


def _ffn_kernel(idx_ref, x_in_ref, xs_ref, val_ref, wg_ref, wu_ref, wd_ref, x_ref, acc_ref, sem, *, cap, tn):
    del x_in_ref
    f = pl.program_id(2)
    rows = acc_ref.shape[0]
    base = pl.program_id(0) * cap + pl.program_id(1) * rows

    @pl.when(f == 0)
    def _():
        _move_rows(idx_ref, base, rows, x_ref, acc_ref, sem, True)

    xs = xs_ref[...]
    hg = jnp.dot(xs, wg_ref[0], preferred_element_type=F32)
    hu = jnp.dot(xs, wu_ref[0], preferred_element_type=F32)
    hh = (hg * jax.nn.sigmoid(hg) * hu * val_ref[0]).astype(BF16)
    for n0 in range(0, acc_ref.shape[1], tn):
        acc_ref[:, n0:n0 + tn] += jnp.dot(hh, wd_ref[0, :, n0:n0 + tn], preferred_element_type=F32)

    @pl.when(f == pl.num_programs(2) - 1)
    def _():
        _move_rows(idx_ref, base, rows, x_ref, acc_ref, sem, False)


def expert_ffn(idx_flat, vals, xs, x, w_gate, w_up, w_down, cap, rows=1024, tf=256, tn=512):
    n_tok, d = x.shape
    ne, _, ff = w_gate.shape
    rows = min(rows, cap)
    rt = cap // rows
    grid_spec = pltpu.PrefetchScalarGridSpec(
        num_scalar_prefetch=1,
        grid=(ne, rt, ff // tf),
        in_specs=[
            pl.BlockSpec(memory_space=pl.ANY),
            pl.BlockSpec((rows, d), lambda e, r, f, idx: (e * rt + r, 0)),
            pl.BlockSpec((1, rows, 1), lambda e, r, f, idx: (e, r, 0)),
            pl.BlockSpec((1, d, tf), lambda e, r, f, idx: (e, 0, f)),
            pl.BlockSpec((1, d, tf), lambda e, r, f, idx: (e, 0, f)),
            pl.BlockSpec((1, tf, d), lambda e, r, f, idx: (e, f, 0)),
        ],
        out_specs=pl.BlockSpec(memory_space=pl.ANY),
        scratch_shapes=[pltpu.VMEM((rows, d), F32), pltpu.SemaphoreType.DMA(())],
    )
    return pl.pallas_call(
        functools.partial(_ffn_kernel, cap=cap, tn=tn),
        grid_spec=grid_spec,
        out_shape=jax.ShapeDtypeStruct((n_tok, d), F32),
        input_output_aliases={1: 0},
        compiler_params=_cparams(("arbitrary", "arbitrary", "arbitrary")),
        name="expert_ffn",
    )(idx_flat, x, xs, vals, w_gate, w_up, w_down)


def _rope_tables(seq_len):
    axis_dim = HEAD_DIM // 2
    rows = seq_len // GRID_W
    row_idx = jnp.repeat(jnp.arange(rows, dtype=F32), GRID_W)
    col_idx = jnp.tile(jnp.arange(GRID_W, dtype=F32), rows)
    inv_freq = ROPE_THETA ** (-jnp.arange(0, axis_dim, 2, dtype=F32) / axis_dim)
    ang_r = row_idx[:, None] * inv_freq[None, :]
    ang_c = col_idx[:, None] * inv_freq[None, :]
    cos = jnp.concatenate([jnp.cos(ang_r), jnp.cos(ang_r), jnp.cos(ang_c), jnp.cos(ang_c)], axis=-1)
    sin = jnp.concatenate([-jnp.sin(ang_r), jnp.sin(ang_r), -jnp.sin(ang_c), jnp.sin(ang_c)], axis=-1)
    return cos, sin


def _prepare_weights(p):
    w_in = p["w_in"]
    w = {
        "w_main": jnp.concatenate([w_in[:, :, :OFF_IF], w_in[:, :, OFF_IF + N_IF:]], axis=-1).astype(BF16),
        "w_if": jnp.pad(w_in[:, :, OFF_IF:OFF_IF + N_IF], ((0, 0), (0, 0), (0, LANES - N_IF))).astype(BF16),
        "w_a": p["w_br_attn"].astype(BF16),
        "w_m": p["w_br_mlstm"].astype(BF16),
        "w_out": p["w_out"].astype(BF16),
        "w_r": jnp.pad(p["w_router"], ((0, 0), (0, 0), (0, LANES - N_EXPERTS))).astype(BF16),
        "w_eg": p["w_e_gate"].astype(BF16),
        "w_eu": p["w_e_up"].astype(BF16),
        "w_ed": p["w_e_down"].astype(BF16),
    }
    gq = p["g_q"].astype(F32) * (HEAD_DIM ** -0.5)
    gk = p["g_k"].astype(F32)
    w["g_qk"] = jnp.concatenate([jnp.repeat(gq[:, None, :], N_Q_HEADS, axis=1),
                                 jnp.repeat(gk[:, None, :], N_KV_HEADS, axis=1)], axis=1)
    bias = jnp.concatenate([p["b_igate"].reshape(-1, 2 * N_M_HEADS), p["b_fgate"].reshape(-1, 2 * N_M_HEADS)], axis=1)
    w["gate_bias"] = jnp.broadcast_to(bias.astype(F32)[:, :, None], bias.shape + (LANES,))
    return w


def _layer(x, l, p, w, cos_tab, sin_tab):
    s_len = x.shape[0]
    xn = rmsnorm(x, p["g_mix"][l], BF16)
    u = matmul(xn, w["w_main"][l], BF16)
    gates_if = matmul(xn, w["w_if"][l], F32)
    qk = qk_prep(u, w["g_qk"][l], cos_tab, sin_tab)
    attn = flash_attention(qk, u)
    hd = mlstm_scan(u, gates_if[:, :N_IF].T, w["gate_bias"][l])
    mh = mlstm_post(hd, u, p["g_mhead"][l])
    merged = branch_merge(attn, mh, w["w_a"][l], w["w_m"][l], u)
    x = residual_matmul(merged, w["w_out"][l], x)

    cap = CAPACITY_FACTOR * s_len // N_EXPERTS
    aff = router(x, p["g_ffn"][l], w["w_r"][l])
    aff3 = aff[:, :N_EXPERTS].T.reshape(N_EXPERTS, s_len // LANES, LANES)
    idx, vals = expert_select(aff3, cap)
    idx_flat = idx.reshape(-1)
    xs = gather_norm(idx_flat, x, p["g_ffn"][l])
    x = expert_ffn(idx_flat, vals, xs, x, w["w_eg"][l], w["w_eu"][l], w["w_ed"][l], cap)
    return x


def _trunk(x, p, w):
    s_len = x.shape[0]
    cos_tab, sin_tab = _rope_tables(s_len)
    for l in range(p["g_mix"].shape[0]):
        x = _layer(x, l, p, w, cos_tab, sin_tab)
    return rmsnorm(x, p["g_final"], F32)


def kernel(x_prompt, x_sample, g_mix, w_in, g_q, g_k, b_igate, b_fgate, g_mhead, w_br_attn, w_br_mlstm, w_out,
           g_ffn, w_router, w_e_gate, w_e_up, w_e_down, g_final):
    p = dict(g_mix=g_mix, w_in=w_in, g_q=g_q, g_k=g_k, b_igate=b_igate, b_fgate=b_fgate, g_mhead=g_mhead,
             w_br_attn=w_br_attn, w_br_mlstm=w_br_mlstm, w_out=w_out, g_ffn=g_ffn, w_router=w_router,
             w_e_gate=w_e_gate, w_e_up=w_e_up, w_e_down=w_e_down, g_final=g_final)
    w = _prepare_weights(p)
    outs = []
    for x in (x_prompt, x_sample):
        b, s_len, d = x.shape
        assert b == 1, "tokens of different batch entries must not attend to each other"
        y = _trunk(x.reshape(b * s_len, d), p, w)
        outs.append(y.reshape(b, s_len, d))
    return tuple(outs)
```

```python
import functools

import jax
import jax.numpy as jnp
import numpy as np
from jax import lax
from jax.experimental import pallas as pl
from jax.experimental.pallas import tpu as pltpu

F32 = jnp.float32
BF16 = jnp.bfloat16
I32 = jnp.int32

GRID_W = 64
HEAD_DIM = 128
N_Q_HEADS = 16
N_KV_HEADS = 4
Q_PER_KV = N_Q_HEADS // N_KV_HEADS
ATTN_WIDTH = N_Q_HEADS * HEAD_DIM
KV_WIDTH = N_KV_HEADS * HEAD_DIM
ROPE_THETA = 10000.0
N_M_HEADS = 8
M_QK_DIM = 128
M_V_DIM = 256
M_QK_WIDTH = N_M_HEADS * M_QK_DIM
M_WIDTH = N_M_HEADS * M_V_DIM
N_EXPERTS = 16
CAPACITY_FACTOR = 2
NORM_EPS = 1e-6

OFF_Q = 0
OFF_K = OFF_Q + ATTN_WIDTH
OFF_V = OFF_K + KV_WIDTH
OFF_MQ = OFF_V + KV_WIDTH
OFF_MK = OFF_MQ + M_QK_WIDTH
OFF_MV = OFF_MK + M_QK_WIDTH
OFF_MO = OFF_MV + M_WIDTH
OFF_IF = OFF_MO + M_WIDTH
N_IF = 4 * N_M_HEADS
OFF_G = OFF_IF

LANES = 128
VMEM_LIMIT = 56 * 1024 * 1024

M_CHUNK = 128


def _cparams(sem, vmem=VMEM_LIMIT):
    return pltpu.CompilerParams(dimension_semantics=sem, vmem_limit_bytes=vmem)


def _rmsnorm_kernel(x_ref, g_ref, o_ref):
    x = x_ref[...].astype(F32)
    ms = jnp.mean(x * x, axis=-1, keepdims=True)
    o_ref[...] = (x * lax.rsqrt(ms + NORM_EPS) * g_ref[...]).astype(o_ref.dtype)


def rmsnorm(x, g, out_dtype, tm=512):
    m, d = x.shape
    return pl.pallas_call(
        _rmsnorm_kernel,
        grid=(m // tm,),
        in_specs=[pl.BlockSpec((tm, d), lambda i: (i, 0)), pl.BlockSpec((1, d), lambda i: (0, 0))],
        out_specs=pl.BlockSpec((tm, d), lambda i: (i, 0)),
        out_shape=jax.ShapeDtypeStruct((m, d), out_dtype),
        compiler_params=_cparams(("parallel",)),
        name="rmsnorm",
    )(x, g.reshape(1, d).astype(F32))


def _mm_kernel(a_ref, b_ref, o_ref):
    o_ref[...] = jnp.dot(a_ref[...], b_ref[...], preferred_element_type=F32).astype(o_ref.dtype)


def _mm2_kernel(a_ref, b1_ref, b2_ref, o_ref, *, nb1):
    a = a_ref[...]

    @pl.when(pl.program_id(1) < nb1)
    def _():
        o_ref[...] = jnp.dot(a, b1_ref[...], preferred_element_type=F32).astype(o_ref.dtype)

    @pl.when(pl.program_id(1) >= nb1)
    def _():
        o_ref[...] = jnp.dot(a, b2_ref[...], preferred_element_type=F32).astype(o_ref.dtype)


def matmul_two_rhs(a, b1, n1, b2, out_dtype, tm=1024, tn=512):
    m, k = a.shape
    n2 = b2.shape[1]
    tm = min(tm, m)
    nb1, nb2 = n1 // tn, n2 // tn
    return pl.pallas_call(
        functools.partial(_mm2_kernel, nb1=nb1),
        grid=(m // tm, nb1 + nb2),
        in_specs=[
            pl.BlockSpec((tm, k), lambda i, j: (i, 0)),
            pl.BlockSpec((k, tn), lambda i, j: (0, jnp.minimum(j, nb1 - 1))),
            pl.BlockSpec((k, tn), lambda i, j: (0, jnp.maximum(j - nb1, 0))),
        ],
        out_specs=pl.BlockSpec((tm, tn), lambda i, j: (i, j)),
        out_shape=jax.ShapeDtypeStruct((m, n1 + n2), out_dtype),
        compiler_params=_cparams(("parallel", "arbitrary")),
        name="matmul_two_rhs",
    )(a, b1, b2)


def matmul(a, b, out_dtype, tm=1024, tn=1024):
    m, k = a.shape
    _, n = b.shape
    tm, tn = min(tm, m), min(tn, n)
    return pl.pallas_call(
        _mm_kernel,
        grid=(m // tm, n // tn),
        in_specs=[pl.BlockSpec((tm, k), lambda i, j: (i, 0)), pl.BlockSpec((k, tn), lambda i, j: (0, j))],
        out_specs=pl.BlockSpec((tm, tn), lambda i, j: (i, j)),
        out_shape=jax.ShapeDtypeStruct((m, n), out_dtype),
        compiler_params=_cparams(("parallel", "parallel")),
        name="matmul",
    )(a, b)


def _qkprep_kernel(u_ref, g_ref, c_ref, s_ref, o_ref):
    n_heads = o_ref.shape[1] // HEAD_DIM
    cos = c_ref[...]
    sin = s_ref[...]
    lane = lax.broadcasted_iota(I32, cos.shape, 1)
    first_half = (lane % (HEAD_DIM // 2)) < (HEAD_DIM // 4)
    for h in range(n_heads):
        sl = slice(h * HEAD_DIM, (h + 1) * HEAD_DIM)
        x = u_ref[:, sl].astype(F32)
        ms = jnp.mean(x * x, axis=-1, keepdims=True)
        xn = x * lax.rsqrt(ms + NORM_EPS) * g_ref[h:h + 1, :]
        partner = jnp.where(first_half, pltpu.roll(xn, HEAD_DIM - HEAD_DIM // 4, 1), pltpu.roll(xn, HEAD_DIM // 4, 1))
        o_ref[:, sl] = (xn * cos + partner * sin).astype(o_ref.dtype)


def qk_prep(u, g_tab, cos_tab, sin_tab, tm=512):
    m = u.shape[0]
    w = ATTN_WIDTH + KV_WIDTH
    return pl.pallas_call(
        _qkprep_kernel,
        grid=(m // tm,),
        in_specs=[
            pl.BlockSpec((tm, w), lambda i: (i, 0)),
            pl.BlockSpec((w // HEAD_DIM, HEAD_DIM), lambda i: (0, 0)),
            pl.BlockSpec((tm, HEAD_DIM), lambda i: (i, 0)),
            pl.BlockSpec((tm, HEAD_DIM), lambda i: (i, 0)),
        ],
        out_specs=pl.BlockSpec((tm, w), lambda i: (i, 0)),
        out_shape=jax.ShapeDtypeStruct((m, w), BF16),
        compiler_params=_cparams(("parallel",)),
        name="qk_prep",
    )(u, g_tab, cos_tab, sin_tab)


def _flash_kernel(q_ref, k_ref, v_ref, o_ref, m_ref, l_ref, acc_ref):
    ki = pl.program_id(2)
    tq = q_ref.shape[0]
    tk = k_ref.shape[0]

    @pl.when(ki == 0)
    def _():
        m_ref[...] = jnp.full(m_ref.shape, -jnp.inf, F32)
        l_ref[...] = jnp.zeros(l_ref.shape, F32)
        acc_ref[...] = jnp.zeros(acc_ref.shape, F32)

    k = k_ref[...]
    v = v_ref[...]
    for g in range(Q_PER_KV):
        rs = slice(g * tq, (g + 1) * tq)
        q = q_ref[:, g * HEAD_DIM:(g + 1) * HEAD_DIM]
        s = lax.dot_general(q, k, (((1,), (1,)), ((), ())), preferred_element_type=F32)
        m_prev = m_ref[rs, :]
        m_new = jnp.maximum(m_prev, jnp.max(s, axis=-1, keepdims=True))
        alpha = jnp.exp2(m_prev - m_new)
        p = jnp.exp2(s - jnp.tile(m_new, (1, tk // LANES)))
        l_ref[rs, :] = alpha * l_ref[rs, :] + jnp.sum(p, axis=-1, keepdims=True)
        acc_ref[rs, :] = alpha * acc_ref[rs, :] + jnp.dot(p.astype(BF16), v, preferred_element_type=F32)
        m_ref[rs, :] = m_new

    @pl.when(ki == pl.num_programs(2) - 1)
    def _():
        for g in range(Q_PER_KV):
            rs = slice(g * tq, (g + 1) * tq)
            o_ref[:, g * HEAD_DIM:(g + 1) * HEAD_DIM] = (acc_ref[rs, :] / l_ref[rs, :]).astype(o_ref.dtype)


def flash_attention(qk, u, tq=512, tk=2048):
    s_len = qk.shape[0]
    tq, tk = min(tq, s_len), min(tk, s_len)
    gw = Q_PER_KV * HEAD_DIM
    rows = Q_PER_KV * tq
    return pl.pallas_call(
        _flash_kernel,
        grid=(N_KV_HEADS, s_len // tq, s_len // tk),
        in_specs=[
            pl.BlockSpec((tq, gw), lambda h, i, j: (i, h)),
            pl.BlockSpec((tk, HEAD_DIM), lambda h, i, j: (j, ATTN_WIDTH // HEAD_DIM + h)),
            pl.BlockSpec((tk, HEAD_DIM), lambda h, i, j: (j, OFF_V // HEAD_DIM + h)),
        ],
        out_specs=pl.BlockSpec((tq, gw), lambda h, i, j: (i, h)),
        out_shape=jax.ShapeDtypeStruct((s_len, ATTN_WIDTH), BF16),
        scratch_shapes=[
            pltpu.VMEM((rows, LANES), F32),
            pltpu.VMEM((rows, LANES), F32),
            pltpu.VMEM((rows, HEAD_DIM), F32),
        ],
        compiler_params=_cparams(("parallel", "parallel", "arbitrary")),
        name="flash_attention",
    )(qk, qk, u)


def _log_sigmoid(x):
    return jnp.minimum(x, 0.0) - jnp.log1p(jnp.exp(-jnp.abs(x)))


def _mlstm_kernel(q_ref, k_ref, v0_ref, v1_ref, gate_ref, bias_ref, o_ref, ct_ref, n_ref, m_ref):
    d = pl.program_id(0)
    c = pl.program_id(1)
    L = q_ref.shape[0]
    H = N_M_HEADS

    @pl.when(c == 0)
    def _():
        ct_ref[...] = jnp.zeros(ct_ref.shape, F32)
        n_ref[...] = jnp.zeros(n_ref.shape, F32)
        m_ref[...] = jnp.zeros(m_ref.shape, F32)

    r_i = pl.multiple_of(d * H, H)
    r_f = pl.multiple_of(2 * H + d * H, H)
    i_rows = gate_ref[pl.ds(r_i, H), :] + bias_ref[pl.ds(r_i, H), :]
    f_rows = _log_sigmoid(gate_ref[pl.ds(r_f, H), :] + bias_ref[pl.ds(r_f, H), :])

    t_idx = lax.broadcasted_iota(I32, (L, L), 0)
    s_idx = lax.broadcasted_iota(I32, (L, L), 1)
    ahead = (s_idx - t_idx) * (1 - 2 * d)
    valid = ahead <= 0
    csum = jnp.where(ahead >= 0, 1.0, 0.0).astype(F32)
    b_rows = jnp.dot(f_rows, csum, preferred_element_type=F32, precision=lax.Precision.HIGHEST)
    totals = jnp.sum(f_rows, axis=-1, keepdims=True)
    g_rows = totals - b_rows + i_rows

    for h in range(H):
        b_row = b_rows[h:h + 1, :]
        i_row = i_rows[h:h + 1, :]
        total = totals[h:h + 1, :]
        b_col = jnp.transpose(jnp.broadcast_to(b_row, (L, L)))
        g_col = jnp.transpose(jnp.broadcast_to(g_rows[h:h + 1, :], (L, L)))[:, 0:1]

        m_prev = m_ref[h:h + 1, 0:1]
        dmat = jnp.where(valid, b_col - b_row + i_row, -jnp.inf)
        inter = b_col[:, 0:1] + m_prev
        m_t = jnp.maximum(inter, jnp.max(dmat, axis=-1, keepdims=True))
        dexp = jnp.exp(dmat - m_t)

        qb = q_ref[:, h * M_QK_DIM:(h + 1) * M_QK_DIM]
        kb = (k_ref[:, h * M_QK_DIM:(h + 1) * M_QK_DIM].astype(F32) * (M_QK_DIM ** -0.5)).astype(BF16)
        v_ref = v0_ref if h < H // 2 else v1_ref
        hv = h % (H // 2)
        vb = v_ref[:, hv * M_V_DIM:(hv + 1) * M_V_DIM]
        qk = lax.dot_general(qb, kb, (((1,), (1,)), ((), ())), preferred_element_type=F32)
        w = dexp * qk
        a_int = jnp.exp(inter - m_t)
        ct = ct_ref[h]
        num = (jnp.dot(w.astype(BF16), vb, preferred_element_type=F32)
               + a_int * jnp.dot(qb, ct.astype(BF16), preferred_element_type=F32))
        n_row = n_ref[h:h + 1, :]
        den = jnp.sum(w, axis=-1, keepdims=True) + a_int * jnp.sum(qb.astype(F32) * n_row, axis=-1, keepdims=True)
        o_ref[:, h * M_V_DIM:(h + 1) * M_V_DIM] = (num / jnp.maximum(jnp.abs(den), jnp.exp(-m_t))).astype(o_ref.dtype)

        m_new = jnp.maximum(total + m_prev, jnp.max(g_col, axis=0, keepdims=True))
        decay = jnp.exp(total + m_prev - m_new)
        kw = kb.astype(F32) * jnp.exp(g_col - m_new)
        ct_ref[h] = decay * ct + lax.dot_general(kw.astype(BF16), vb, (((0,), (0,)), ((), ())),
                                                 preferred_element_type=F32)
        n_ref[h:h + 1, :] = decay * n_row + jnp.sum(kw, axis=0, keepdims=True)
        m_ref[h:h + 1, :] = jnp.broadcast_to(m_new, (1, m_ref.shape[1]))


def mlstm_scan(u, gates_t, bias_tab):
    s_len = u.shape[0]
    L = M_CHUNK
    nc = s_len // L
    half_v = M_WIDTH // 2

    def cidx(d, c):
        return c + d * (nc - 1 - 2 * c)

    return pl.pallas_call(
        _mlstm_kernel,
        grid=(2, nc),
        in_specs=[
            pl.BlockSpec((L, M_QK_WIDTH), lambda d, c: (cidx(d, c), OFF_MQ // M_QK_WIDTH)),
            pl.BlockSpec((L, M_QK_WIDTH), lambda d, c: (cidx(d, c), OFF_MK // M_QK_WIDTH)),
            pl.BlockSpec((L, half_v), lambda d, c: (cidx(d, c), OFF_MV // half_v)),
            pl.BlockSpec((L, half_v), lambda d, c: (cidx(d, c), OFF_MV // half_v + 1)),
            pl.BlockSpec((N_IF, L), lambda d, c: (0, cidx(d, c))),
            pl.BlockSpec((N_IF, LANES), lambda d, c: (0, 0)),
        ],
        out_specs=pl.BlockSpec((None, L, M_WIDTH), lambda d, c: (d, cidx(d, c), 0)),
        out_shape=jax.ShapeDtypeStruct((2, s_len, M_WIDTH), F32),
        scratch_shapes=[
            pltpu.VMEM((N_M_HEADS, M_QK_DIM, M_V_DIM), F32),
            pltpu.VMEM((N_M_HEADS, M_QK_DIM), F32),
            pltpu.VMEM((N_M_HEADS, LANES), F32),
        ],
        compiler_params=_cparams(("parallel", "arbitrary")),
        name="mlstm_scan",
    )(u, u, u, u, gates_t, bias_tab)


def _mlstm_post_kernel(h_ref, o_ref, g_ref, out_ref):
    hsum = h_ref[0] + h_ref[1]
    hn = hsum * lax.rsqrt(jnp.mean(hsum * hsum, axis=-1, keepdims=True) + NORM_EPS)
    out_ref[...] = (hn * g_ref[...] * jax.nn.sigmoid(o_ref[...].astype(F32))).astype(out_ref.dtype)


def mlstm_post(hd, u, g_mh, tm=512):
    s_len = u.shape[0]
    tm = min(tm, s_len)
    return pl.pallas_call(
        _mlstm_post_kernel,
        grid=(s_len // tm, N_M_HEADS),
        in_specs=[
            pl.BlockSpec((2, tm, M_V_DIM), lambda i, h: (0, i, h)),
            pl.BlockSpec((tm, M_V_DIM), lambda i, h: (i, OFF_MO // M_V_DIM + h)),
            pl.BlockSpec((1, M_V_DIM), lambda i, h: (0, h)),
        ],
        out_specs=pl.BlockSpec((tm, M_V_DIM), lambda i, h: (i, h)),
        out_shape=jax.ShapeDtypeStruct((s_len, M_WIDTH), BF16),
        compiler_params=_cparams(("parallel", "parallel")),
        name="mlstm_post",
    )(hd, u, g_mh.reshape(1, M_WIDTH).astype(F32))


def _merge_kernel(a_ref, m_ref, wa_ref, wm_ref, ga_ref, gm_ref, o_ref):
    ya = jnp.dot(a_ref[...], wa_ref[...], preferred_element_type=F32)
    ym = jnp.dot(m_ref[...], wm_ref[...], preferred_element_type=F32)
    ga = jax.nn.sigmoid(ga_ref[...].astype(F32))
    gm = jax.nn.sigmoid(gm_ref[...].astype(F32))
    o_ref[...] = (ga * ya + gm * ym).astype(o_ref.dtype)


def branch_merge(attn, mh, w_a, w_m, u, tm=1024, tn=1024):
    m, ka = attn.shape
    km = mh.shape[1]
    d = w_a.shape[1]
    tm = min(tm, m)
    gblk = OFF_G // tn
    return pl.pallas_call(
        _merge_kernel,
        grid=(m // tm, d // tn),
        in_specs=[
            pl.BlockSpec((tm, ka), lambda i, j: (i, 0)),
            pl.BlockSpec((tm, km), lambda i, j: (i, 0)),
            pl.BlockSpec((ka, tn), lambda i, j: (0, j)),
            pl.BlockSpec((km, tn), lambda i, j: (0, j)),
            pl.BlockSpec((tm, tn), lambda i, j: (i, gblk + j)),
            pl.BlockSpec((tm, tn), lambda i, j: (i, gblk + d // tn + j)),
        ],
        out_specs=pl.BlockSpec((tm, tn), lambda i, j: (i, j)),
        out_shape=jax.ShapeDtypeStruct((m, d), BF16),
        compiler_params=_cparams(("parallel", "parallel")),
        name="branch_merge",
    )(attn, mh, w_a, w_m, u, u)


def _resid_mm_kernel(a_ref, w_ref, x_ref, o_ref):
    o_ref[...] = x_ref[...] + jnp.dot(a_ref[...], w_ref[...], preferred_element_type=F32)


def residual_matmul(a, w, x, tm=1024, tn=1024):
    m, k = a.shape
    n = w.shape[1]
    tm = min(tm, m)
    return pl.pallas_call(
        _resid_mm_kernel,
        grid=(m // tm, n // tn),
        in_specs=[
            pl.BlockSpec((tm, k), lambda i, j: (i, 0)),
            pl.BlockSpec((k, tn), lambda i, j: (0, j)),
            pl.BlockSpec((tm, tn), lambda i, j: (i, j)),
        ],
        out_specs=pl.BlockSpec((tm, tn), lambda i, j: (i, j)),
        out_shape=jax.ShapeDtypeStruct((m, n), F32),
        input_output_aliases={2: 0},
        compiler_params=_cparams(("parallel", "parallel")),
        name="residual_matmul",
    )(a, w, x)


def _router_kernel(x_ref, g_ref, w_ref, o_ref):
    x = x_ref[...]
    ms = jnp.mean(x * x, axis=-1, keepdims=True)
    xn = (x * lax.rsqrt(ms + NORM_EPS) * g_ref[...]).astype(BF16)
    logits = jnp.dot(xn, w_ref[...], preferred_element_type=F32)
    lane = lax.broadcasted_iota(I32, logits.shape, 1)
    logits = jnp.where(lane < N_EXPERTS, logits, -jnp.inf)
    e = jnp.exp(logits - jnp.max(logits, axis=-1, keepdims=True))
    o_ref[...] = e / jnp.sum(e, axis=-1, keepdims=True)


def router(x, g, w_pad, tm=512):
    m, d = x.shape
    tm = min(tm, m)
    return pl.pallas_call(
        _router_kernel,
        grid=(m // tm,),
        in_specs=[
            pl.BlockSpec((tm, d), lambda i: (i, 0)),
            pl.BlockSpec((1, d), lambda i: (0, 0)),
            pl.BlockSpec((d, LANES), lambda i: (0, 0)),
        ],
        out_specs=pl.BlockSpec((tm, LANES), lambda i: (i, 0)),
        out_shape=jax.ShapeDtypeStruct((m, LANES), F32),
        compiler_params=_cparams(("parallel",)),
        name="router",
    )(x, g.reshape(1, d).astype(F32), w_pad)


def _select_kernel(aff_ref, idx_ref, val_ref, *, cap):
    a = aff_ref[0]
    nb = a.shape[0]
    bits = pltpu.bitcast(a, I32)
    capf = jnp.float32(cap)

    def count(mask):
        return jnp.sum(jnp.sum(mask.astype(F32), axis=1, keepdims=True), axis=0, keepdims=True)

    def body(i, t):
        cand = t | jnp.left_shift(jnp.int32(1), jnp.int32(30) - i)
        return jnp.where(count(bits >= cand) >= capf, cand, t)

    thr = lax.fori_loop(0, 31, body, jnp.zeros((1, 1), I32))
    gt = bits > thr
    eq = bits == thr
    need = capf - count(gt)

    k_i = lax.broadcasted_iota(I32, (LANES, LANES), 0)
    s_i = lax.broadcasted_iota(I32, (LANES, LANES), 1)
    incl = (k_i <= s_i).astype(BF16)
    rb = lax.broadcasted_iota(I32, (nb, nb), 0)
    cb = lax.broadcasted_iota(I32, (nb, nb), 1)
    blk_strict = (cb < rb).astype(BF16)
    blk_incl = (rb <= cb).astype(BF16)

    eq_b = eq.astype(BF16)
    eq_in = jnp.dot(eq_b, incl, preferred_element_type=F32)
    eq_tot = jnp.broadcast_to(eq_in[:, LANES - 1:LANES], (nb, LANES)).astype(BF16)
    eq_off = jnp.dot(blk_strict, eq_tot, preferred_element_type=F32)
    eq_before = eq_off + eq_in - eq.astype(F32)
    sel = gt | (eq & (eq_before < need))

    sel_b = sel.astype(BF16)
    cin = jnp.dot(sel_b, incl, preferred_element_type=F32)
    tot_row = lax.dot_general(jnp.ones((8, LANES), BF16), sel_b, (((1,), (1,)), ((), ())),
                              preferred_element_type=F32)
    binc_row = jnp.dot(tot_row.astype(BF16), blk_incl, preferred_element_type=F32)[0:1, :]
    bexc_row = binc_row - tot_row[0:1, :]

    j_col = lax.broadcasted_iota(I32, (cap, 1), 0).astype(F32)
    kj = jnp.sum((binc_row <= j_col).astype(F32), axis=1, keepdims=True)
    blk_lane = lax.broadcasted_iota(I32, (cap, nb), 1).astype(F32)
    onehot = blk_lane == kj
    rows_c = jnp.dot(onehot.astype(BF16), cin.astype(BF16), preferred_element_type=F32)
    bex_k = jnp.sum(jnp.where(onehot, bexc_row, 0.0), axis=1, keepdims=True)
    r = j_col - bex_k
    pos = jnp.sum((rows_c <= r).astype(F32), axis=1, keepdims=True)
    idx_ref[0] = (kj * LANES + pos).astype(I32)
    rows_a = jnp.dot(onehot.astype(F32), a, preferred_element_type=F32, precision=lax.Precision.HIGHEST)
    pos_lane = lax.broadcasted_iota(I32, (cap, LANES), 1).astype(F32)
    val_ref[0] = jnp.sum(jnp.where(pos_lane == pos, rows_a, 0.0), axis=1, keepdims=True)


def expert_select(aff3, cap):
    ne, nb, _ = aff3.shape
    return pl.pallas_call(
        functools.partial(_select_kernel, cap=cap),
        grid=(ne,),
        in_specs=[pl.BlockSpec((1, nb, LANES), lambda e: (e, 0, 0))],
        out_specs=[pl.BlockSpec((1, cap, 1), lambda e: (e, 0, 0)), pl.BlockSpec((1, cap, 1), lambda e: (e, 0, 0))],
        out_shape=[jax.ShapeDtypeStruct((ne, cap, 1), I32), jax.ShapeDtypeStruct((ne, cap, 1), F32)],
        compiler_params=_cparams(("parallel",)),
        name="expert_select",
    )(aff3)


def _row_copy(idx_ref, slot, r, hbm_ref, vmem_ref, sem, to_vmem):
    hbm = hbm_ref.at[pl.ds(idx_ref[slot], 1), :]
    vm = vmem_ref.at[pl.ds(r, 1), :]
    return pltpu.make_async_copy(hbm, vm, sem) if to_vmem else pltpu.make_async_copy(vm, hbm, sem)


ROW_DMA_UNROLL = 8


def _start_rows(idx_ref, base, rows, hbm_ref, vmem_ref, sem, to_vmem):
    def body(i, carry):
        r0 = pl.multiple_of(i * ROW_DMA_UNROLL, ROW_DMA_UNROLL)
        for j in range(ROW_DMA_UNROLL):
            _row_copy(idx_ref, base + r0 + j, r0 + j, hbm_ref, vmem_ref, sem, to_vmem).start()
        return carry

    lax.fori_loop(0, rows // ROW_DMA_UNROLL, body, 0)


def _wait_rows(idx_ref, base, rows, hbm_ref, vmem_ref, sem, to_vmem):
    def body(r, carry):
        _row_copy(idx_ref, base + r, r, hbm_ref, vmem_ref, sem, to_vmem).wait()
        return carry

    lax.fori_loop(0, rows, body, 0, unroll=ROW_DMA_UNROLL)


def _gather_norm_kernel(idx_ref, x_ref, g_ref, o_ref, buf_ref, sem):
    i = pl.program_id(0)
    rows = buf_ref.shape[1]
    slot = i % 2

    @pl.when(i == 0)
    def _():
        _start_rows(idx_ref, 0, rows, x_ref, buf_ref.at[0], sem.at[0], True)

    @pl.when(i + 1 < pl.num_programs(0))
    def _():
        _start_rows(idx_ref, (i + 1) * rows, rows, x_ref, buf_ref.at[1 - slot], sem.at[1 - slot], True)

    _wait_rows(idx_ref, i * rows, rows, x_ref, buf_ref.at[slot], sem.at[slot], True)
    x = buf_ref[slot]
    ms = jnp.mean(x * x, axis=-1, keepdims=True)
    o_ref[...] = (x * lax.rsqrt(ms + NORM_EPS) * g_ref[...]).astype(o_ref.dtype)


def gather_norm(idx_flat, x, g, rows=512):
    n_slots = idx_flat.shape[0]
    d = x.shape[1]
    rows = min(rows, n_slots)
    grid_spec = pltpu.PrefetchScalarGridSpec(
        num_scalar_prefetch=1,
        grid=(n_slots // rows,),
        in_specs=[pl.BlockSpec(memory_space=pl.ANY), pl.BlockSpec((1, d), lambda i, idx: (0, 0))],
        out_specs=pl.BlockSpec((rows, d), lambda i, idx: (i, 0)),
        scratch_shapes=[pltpu.VMEM((2, rows, d), F32), pltpu.SemaphoreType.DMA((2,))],
    )
    return pl.pallas_call(
        _gather_norm_kernel,
        grid_spec=grid_spec,
        out_shape=jax.ShapeDtypeStruct((n_slots, d), BF16),
        compiler_params=_cparams(("arbitrary",)),
        name="gather_norm",
    )(idx_flat, x, g.reshape(1, d).astype(F32))


FFN_GATHER_STEP = 1


def _ffn_kernel(idx_ref, x_in_ref, xs_ref, val_ref, wg_ref, wu_ref, wd_ref, x_ref, acc_ref, xbuf_ref, sems, *, cap, tn):
    del x_in_ref
    e, r, f = pl.program_id(0), pl.program_id(1), pl.program_id(2)
    n_f = pl.num_programs(2)
    rows = acc_ref.shape[0]
    base = e * cap + r * rows
    tile = e * pl.num_programs(1) + r
    g_sem, s_sem = sems.at[0], sems.at[1]

    @pl.when(f == FFN_GATHER_STEP)
    def _():
        @pl.when(tile > 0)
        def _():
            _wait_rows(idx_ref, base - rows, rows, x_ref, xbuf_ref, s_sem, False)
        _start_rows(idx_ref, base, rows, x_ref, xbuf_ref, g_sem, True)

    xs = xs_ref[...]
    hg = jnp.dot(xs, wg_ref[0], preferred_element_type=F32)
    hu = jnp.dot(xs, wu_ref[0], preferred_element_type=F32)
    hh = (hg * jax.nn.sigmoid(hg) * hu * val_ref[0]).astype(BF16)

    @pl.when(f == 0)
    def _():
        acc_ref[...] = jnp.zeros(acc_ref.shape, F32)

    for n0 in range(0, acc_ref.shape[1], tn):
        acc_ref[:, n0:n0 + tn] += jnp.dot(hh, wd_ref[0, :, n0:n0 + tn], preferred_element_type=F32)

    @pl.when(f == n_f - 1)
    def _():
        _wait_rows(idx_ref, base, rows, x_ref, xbuf_ref, g_sem, True)
        xbuf_ref[...] += acc_ref[...]
        _start_rows(idx_ref, base, rows, x_ref, xbuf_ref, s_sem, False)

        @pl.when(tile == pl.num_programs(0) * pl.num_programs(1) - 1)
        def _():
            _wait_rows(idx_ref, base, rows, x_ref, xbuf_ref, s_sem, False)


def expert_ffn(idx_flat, vals, xs, x, w_gate, w_up, w_down, cap, rows=512, tf=256, tn=512):
    n_tok, d = x.shape
    ne, _, ff = w_gate.shape
    rows = min(rows, cap)
    rt = cap // rows
    assert FFN_GATHER_STEP < ff // tf - 1
    grid_spec = pltpu.PrefetchScalarGridSpec(
        num_scalar_prefetch=1,
        grid=(ne, rt, ff // tf),
        in_specs=[
            pl.BlockSpec(memory_space=pl.ANY),
            pl.BlockSpec((rows, d), lambda e, r, f, idx: (e * rt + r, 0)),
            pl.BlockSpec((1, rows, 1), lambda e, r, f, idx: (e, r, 0)),
            pl.BlockSpec((1, d, tf), lambda e, r, f, idx: (e, 0, f)),
            pl.BlockSpec((1, d, tf), lambda e, r, f, idx: (e, 0, f)),
            pl.BlockSpec((1, tf, d), lambda e, r, f, idx: (e, f, 0)),
        ],
        out_specs=pl.BlockSpec(memory_space=pl.ANY),
        scratch_shapes=[pltpu.VMEM((rows, d), F32), pltpu.VMEM((rows, d), F32), pltpu.SemaphoreType.DMA((2,))],
    )
    return pl.pallas_call(
        functools.partial(_ffn_kernel, cap=cap, tn=tn),
        grid_spec=grid_spec,
        out_shape=jax.ShapeDtypeStruct((n_tok, d), F32),
        input_output_aliases={1: 0},
        compiler_params=_cparams(("arbitrary", "arbitrary", "arbitrary")),
        name="expert_ffn",
    )(idx_flat, x, xs, vals, w_gate, w_up, w_down)


def _rope_tables(seq_len):
    axis_dim = HEAD_DIM // 2
    rows = seq_len // GRID_W
    row_idx = jnp.repeat(jnp.arange(rows, dtype=F32), GRID_W)
    col_idx = jnp.tile(jnp.arange(GRID_W, dtype=F32), rows)
    inv_freq = ROPE_THETA ** (-jnp.arange(0, axis_dim, 2, dtype=F32) / axis_dim)
    ang_r = row_idx[:, None] * inv_freq[None, :]
    ang_c = col_idx[:, None] * inv_freq[None, :]
    cos = jnp.concatenate([jnp.cos(ang_r), jnp.cos(ang_r), jnp.cos(ang_c), jnp.cos(ang_c)], axis=-1)
    sin = jnp.concatenate([-jnp.sin(ang_r), jnp.sin(ang_r), -jnp.sin(ang_c), jnp.sin(ang_c)], axis=-1)
    return cos, sin


def _prepare_weights(p):
    w_in = p["w_in"].astype(BF16)
    w = {
        "w_in": w_in,
        "w_g": w_in[:, :, OFF_IF + N_IF:],
        "w_if": jnp.pad(w_in[:, :, OFF_IF:OFF_IF + N_IF], ((0, 0), (0, 0), (0, LANES - N_IF))),
        "w_a": p["w_br_attn"].astype(BF16),
        "w_m": p["w_br_mlstm"].astype(BF16),
        "w_out": p["w_out"].astype(BF16),
        "w_r": jnp.pad(p["w_router"], ((0, 0), (0, 0), (0, LANES - N_EXPERTS))).astype(BF16),
        "w_eg": p["w_e_gate"].astype(BF16),
        "w_eu": p["w_e_up"].astype(BF16),
        "w_ed": p["w_e_down"].astype(BF16),
    }
    gq = p["g_q"].astype(F32) * (HEAD_DIM ** -0.5 * np.log2(np.e))
    gk = p["g_k"].astype(F32)
    w["g_qk"] = jnp.concatenate([jnp.repeat(gq[:, None, :], N_Q_HEADS, axis=1),
                                 jnp.repeat(gk[:, None, :], N_KV_HEADS, axis=1)], axis=1)
    bias = jnp.concatenate([p["b_igate"].reshape(-1, 2 * N_M_HEADS), p["b_fgate"].reshape(-1, 2 * N_M_HEADS)], axis=1)
    w["gate_bias"] = jnp.broadcast_to(bias.astype(F32)[:, :, None], bias.shape + (LANES,))
    return w


def _mixer(x, l, p, w, cos_tab, sin_tab):
    xn = rmsnorm(x, p["g_mix"][l], BF16)
    u = matmul_two_rhs(xn, w["w_in"][l], OFF_IF, w["w_g"][l], BF16)
    gates_if = matmul(xn, w["w_if"][l], F32)
    qk = qk_prep(u, w["g_qk"][l], cos_tab, sin_tab)
    attn = flash_attention(qk, u)
    hd = mlstm_scan(u, gates_if[:, :N_IF].T, w["gate_bias"][l])
    mh = mlstm_post(hd, u, p["g_mhead"][l])
    merged = branch_merge(attn, mh, w["w_a"][l], w["w_m"][l], u)
    return residual_matmul(merged, w["w_out"][l], x)


def _expert_choice_ffn(x, l, p, w):
    s_len = x.shape[0]
    cap = CAPACITY_FACTOR * s_len // N_EXPERTS
    aff = router(x, p["g_ffn"][l], w["w_r"][l])
    aff3 = aff[:, :N_EXPERTS].T.reshape(N_EXPERTS, s_len // LANES, LANES)
    idx, vals = expert_select(aff3, cap)
    idx_flat = idx.reshape(-1)
    xs = gather_norm(idx_flat, x, p["g_ffn"][l])
    x = expert_ffn(idx_flat, vals, xs, x, w["w_eg"][l], w["w_eu"][l], w["w_ed"][l], cap)
    return x


def _trunk(x, p, w):
    s_len = x.shape[0]
    cos_tab, sin_tab = _rope_tables(s_len)
    for l in range(p["g_mix"].shape[0]):
        x = _mixer(x, l, p, w, cos_tab, sin_tab)
        x = _expert_choice_ffn(x, l, p, w)
    return rmsnorm(x, p["g_final"], F32)


def kernel(x_prompt, x_sample, g_mix, w_in, g_q, g_k, b_igate, b_fgate, g_mhead, w_br_attn, w_br_mlstm, w_out,
           g_ffn, w_router, w_e_gate, w_e_up, w_e_down, g_final):
    p = dict(g_mix=g_mix, w_in=w_in, g_q=g_q, g_k=g_k, b_igate=b_igate, b_fgate=b_fgate, g_mhead=g_mhead,
             w_br_attn=w_br_attn, w_br_mlstm=w_br_mlstm, w_out=w_out, g_ffn=g_ffn, w_router=w_router,
             w_e_gate=w_e_gate, w_e_up=w_e_up, w_e_down=w_e_down, g_final=g_final)
    w = _prepare_weights(p)
    outs = []
    for x in (x_prompt, x_sample):
        b, s_len, d = x.shape
        assert b == 1, "tokens of different batch entries must not attend to each other"
        y = _trunk(x.reshape(b * s_len, d), p, w)
        outs.append(y.reshape(b, s_len, d))
    return tuple(outs)
```

```python
import functools

import jax
import jax.numpy as jnp
import numpy as np
from jax import lax
from jax.experimental import pallas as pl
from jax.experimental.pallas import tpu as pltpu

F32 = jnp.float32
BF16 = jnp.bfloat16
I32 = jnp.int32

GRID_W = 64
HEAD_DIM = 128
N_Q_HEADS = 16
N_KV_HEADS = 4
Q_PER_KV = N_Q_HEADS // N_KV_HEADS
ATTN_WIDTH = N_Q_HEADS * HEAD_DIM
KV_WIDTH = N_KV_HEADS * HEAD_DIM
ROPE_THETA = 10000.0
N_M_HEADS = 8
M_QK_DIM = 128
M_V_DIM = 256
M_QK_WIDTH = N_M_HEADS * M_QK_DIM
M_WIDTH = N_M_HEADS * M_V_DIM
N_EXPERTS = 16
CAPACITY_FACTOR = 2
NORM_EPS = 1e-6

OFF_Q = 0
OFF_K = OFF_Q + ATTN_WIDTH
OFF_V = OFF_K + KV_WIDTH
OFF_MQ = OFF_V + KV_WIDTH
OFF_MK = OFF_MQ + M_QK_WIDTH
OFF_MV = OFF_MK + M_QK_WIDTH
OFF_MO = OFF_MV + M_WIDTH
OFF_IF = OFF_MO + M_WIDTH
N_IF = 4 * N_M_HEADS
OFF_G = OFF_IF

LANES = 128
VMEM_LIMIT = 56 * 1024 * 1024

M_CHUNK = 128


def _cparams(sem, vmem=VMEM_LIMIT):
    return pltpu.CompilerParams(dimension_semantics=sem, vmem_limit_bytes=vmem)


def _rmsnorm_kernel(x_ref, g_ref, o_ref):
    x = x_ref[...].astype(F32)
    ms = jnp.mean(x * x, axis=-1, keepdims=True)
    o_ref[...] = (x * lax.rsqrt(ms + NORM_EPS) * g_ref[...]).astype(o_ref.dtype)


def rmsnorm(x, g, out_dtype, tm=512):
    m, d = x.shape
    return pl.pallas_call(
        _rmsnorm_kernel,
        grid=(m // tm,),
        in_specs=[pl.BlockSpec((tm, d), lambda i: (i, 0)), pl.BlockSpec((1, d), lambda i: (0, 0))],
        out_specs=pl.BlockSpec((tm, d), lambda i: (i, 0)),
        out_shape=jax.ShapeDtypeStruct((m, d), out_dtype),
        compiler_params=_cparams(("parallel",)),
        name="rmsnorm",
    )(x, g.reshape(1, d).astype(F32))


def _mm_kernel(a_ref, b_ref, o_ref):
    o_ref[...] = jnp.dot(a_ref[...], b_ref[...], preferred_element_type=F32).astype(o_ref.dtype)


def matmul(a, b, out_dtype, tm=1024, tn=1024):
    m, k = a.shape
    _, n = b.shape
    tm, tn = min(tm, m), min(tn, n)
    return pl.pallas_call(
        _mm_kernel,
        grid=(m // tm, n // tn),
        in_specs=[pl.BlockSpec((tm, k), lambda i, j: (i, 0)), pl.BlockSpec((k, tn), lambda i, j: (0, j))],
        out_specs=pl.BlockSpec((tm, tn), lambda i, j: (i, j)),
        out_shape=jax.ShapeDtypeStruct((m, n), out_dtype),
        compiler_params=_cparams(("parallel", "parallel")),
        name="matmul",
    )(a, b)


def _qkprep_kernel(u_ref, g_ref, c_ref, s_ref, o_ref):
    n_heads = o_ref.shape[1] // HEAD_DIM
    cos = c_ref[...]
    sin = s_ref[...]
    lane = lax.broadcasted_iota(I32, cos.shape, 1)
    first_half = (lane % (HEAD_DIM // 2)) < (HEAD_DIM // 4)
    for h in range(n_heads):
        sl = slice(h * HEAD_DIM, (h + 1) * HEAD_DIM)
        x = u_ref[:, sl].astype(F32)
        ms = jnp.mean(x * x, axis=-1, keepdims=True)
        xn = x * lax.rsqrt(ms + NORM_EPS) * g_ref[h:h + 1, :]
        partner = jnp.where(first_half, pltpu.roll(xn, HEAD_DIM - HEAD_DIM // 4, 1), pltpu.roll(xn, HEAD_DIM // 4, 1))
        o_ref[:, sl] = (xn * cos + partner * sin).astype(o_ref.dtype)


def qk_prep(u, g_tab, cos_tab, sin_tab, tm=512):
    m = u.shape[0]
    w = ATTN_WIDTH + KV_WIDTH
    return pl.pallas_call(
        _qkprep_kernel,
        grid=(m // tm,),
        in_specs=[
            pl.BlockSpec((tm, w), lambda i: (i, 0)),
            pl.BlockSpec((w // HEAD_DIM, HEAD_DIM), lambda i: (0, 0)),
            pl.BlockSpec((tm, HEAD_DIM), lambda i: (i, 0)),
            pl.BlockSpec((tm, HEAD_DIM), lambda i: (i, 0)),
        ],
        out_specs=pl.BlockSpec((tm, w), lambda i: (i, 0)),
        out_shape=jax.ShapeDtypeStruct((m, w), BF16),
        compiler_params=_cparams(("parallel",)),
        name="qk_prep",
    )(u, g_tab, cos_tab, sin_tab)


def _flash_kernel(q_ref, k_ref, v_ref, o_ref, m_ref, l_ref, acc_ref):
    ki = pl.program_id(2)
    tq = q_ref.shape[0]
    tk = k_ref.shape[0]

    @pl.when(ki == 0)
    def _():
        m_ref[...] = jnp.full(m_ref.shape, -jnp.inf, F32)
        l_ref[...] = jnp.zeros(l_ref.shape, F32)
        acc_ref[...] = jnp.zeros(acc_ref.shape, F32)

    k = k_ref[...]
    v = v_ref[...]
    for g in range(Q_PER_KV):
        rs = slice(g * tq, (g + 1) * tq)
        q = q_ref[:, g * HEAD_DIM:(g + 1) * HEAD_DIM]
        s = lax.dot_general(q, k, (((1,), (1,)), ((), ())), preferred_element_type=F32)
        m_prev = m_ref[rs, :]
        m_new = jnp.maximum(m_prev, jnp.max(s, axis=-1, keepdims=True))
        alpha = jnp.exp2(m_prev - m_new)
        p = jnp.exp2(s - jnp.tile(m_new, (1, tk // LANES)))
        l_ref[rs, :] = alpha * l_ref[rs, :] + jnp.sum(p, axis=-1, keepdims=True)
        acc_ref[rs, :] = alpha * acc_ref[rs, :] + jnp.dot(p.astype(BF16), v, preferred_element_type=F32)
        m_ref[rs, :] = m_new

    @pl.when(ki == pl.num_programs(2) - 1)
    def _():
        for g in range(Q_PER_KV):
            rs = slice(g * tq, (g + 1) * tq)
            o_ref[:, g * HEAD_DIM:(g + 1) * HEAD_DIM] = (acc_ref[rs, :] / l_ref[rs, :]).astype(o_ref.dtype)


def flash_attention(qk, u, tq=512, tk=2048):
    s_len = qk.shape[0]
    tq, tk = min(tq, s_len), min(tk, s_len)
    gw = Q_PER_KV * HEAD_DIM
    rows = Q_PER_KV * tq
    return pl.pallas_call(
        _flash_kernel,
        grid=(N_KV_HEADS, s_len // tq, s_len // tk),
        in_specs=[
            pl.BlockSpec((tq, gw), lambda h, i, j: (i, h)),
            pl.BlockSpec((tk, HEAD_DIM), lambda h, i, j: (j, ATTN_WIDTH // HEAD_DIM + h)),
            pl.BlockSpec((tk, HEAD_DIM), lambda h, i, j: (j, OFF_V // HEAD_DIM + h)),
        ],
        out_specs=pl.BlockSpec((tq, gw), lambda h, i, j: (i, h)),
        out_shape=jax.ShapeDtypeStruct((s_len, ATTN_WIDTH), BF16),
        scratch_shapes=[
            pltpu.VMEM((rows, LANES), F32),
            pltpu.VMEM((rows, LANES), F32),
            pltpu.VMEM((rows, HEAD_DIM), F32),
        ],
        compiler_params=_cparams(("parallel", "parallel", "arbitrary")),
        name="flash_attention",
    )(qk, qk, u)


def _log_sigmoid(x):
    return jnp.minimum(x, 0.0) - jnp.log1p(jnp.exp(-jnp.abs(x)))


def _mlstm_kernel(q_ref, kt_ref, v0_ref, v1_ref, gate_ref, bias_ref, o_ref, st_ref, m_ref):
    d = pl.program_id(0)
    c = pl.program_id(1)
    L = q_ref.shape[0]
    H = N_M_HEADS
    DK, DV = M_QK_DIM, M_V_DIM

    @pl.when(c == 0)
    def _():
        st_ref[...] = jnp.zeros(st_ref.shape, F32)
        m_ref[...] = jnp.zeros(m_ref.shape, F32)

    r_i = pl.multiple_of(d * H, H)
    r_f = pl.multiple_of(2 * H + d * H, H)
    i_rows = gate_ref[pl.ds(r_i, H), :] + bias_ref[pl.ds(r_i, H), :]
    f_rows = _log_sigmoid(gate_ref[pl.ds(r_f, H), :] + bias_ref[pl.ds(r_f, H), :])

    t_idx = lax.broadcasted_iota(I32, (L, L), 0)
    s_idx = lax.broadcasted_iota(I32, (L, L), 1)
    ahead = (s_idx - t_idx) * (1 - 2 * d)
    valid = ahead <= 0
    csum = jnp.where(ahead >= 0, 1.0, 0.0).astype(F32)
    b_rows = jnp.dot(f_rows, csum, preferred_element_type=F32, precision=lax.Precision.HIGHEST)
    totals = jnp.sum(f_rows, axis=-1, keepdims=True)
    g_rows = totals - b_rows + i_rows
    m_prevs = m_ref[...]
    m_news = jnp.maximum(totals + m_prevs, jnp.max(g_rows, axis=-1, keepdims=True))
    decays = jnp.exp(totals + m_prevs - m_news)
    wk_rows = jnp.exp(g_rows - m_news[:, 0:1])
    m_ref[...] = m_news
    ones_col = (lax.broadcasted_iota(I32, (L, LANES), 1) == 0).astype(BF16)

    for h in range(H):
        b_row = b_rows[h:h + 1, :]
        i_row = i_rows[h:h + 1, :]
        b_col = jnp.transpose(jnp.broadcast_to(b_row, (L, L)))

        m_prev = m_prevs[h:h + 1, 0:1]
        dmat = jnp.where(valid, b_col - b_row + i_row, -jnp.inf)
        inter = b_col[:, 0:1] + m_prev
        m_t = jnp.maximum(inter, jnp.max(dmat, axis=-1, keepdims=True))
        dexp = jnp.exp(dmat - m_t)

        qb = q_ref[:, h * DK:(h + 1) * DK]
        kts = (kt_ref[h * DK:(h + 1) * DK, :].astype(F32) * (DK ** -0.5))
        v_ref = v0_ref if h < H // 2 else v1_ref
        hv = h % (H // 2)
        v_ext = jnp.concatenate([v_ref[:, hv * DV:(hv + 1) * DV], ones_col], axis=1)
        qk = jnp.dot(qb, kts.astype(BF16), preferred_element_type=F32)
        w = dexp * qk
        a_int = jnp.exp(inter - m_t)
        st = st_ref[h]
        q_st = jnp.dot(qb, st.astype(BF16), preferred_element_type=F32)
        num = jnp.dot(w.astype(BF16), v_ext[:, :DV], preferred_element_type=F32) + a_int * q_st[:, :DV]
        den = jnp.sum(w, axis=-1, keepdims=True) + a_int * q_st[:, DV:DV + 1]
        o_ref[:, h * DV:(h + 1) * DV] = (num / jnp.maximum(jnp.abs(den), jnp.exp(-m_t))).astype(o_ref.dtype)

        kwt = (kts * wk_rows[h:h + 1, :]).astype(BF16)
        st_ref[h] = decays[h:h + 1, 0:1] * st + jnp.dot(kwt, v_ext, preferred_element_type=F32)


def mlstm_scan(u, kt, gates_t, bias_tab):
    s_len = u.shape[0]
    L = M_CHUNK
    nc = s_len // L
    half_v = M_WIDTH // 2

    def cidx(d, c):
        return c + d * (nc - 1 - 2 * c)

    return pl.pallas_call(
        _mlstm_kernel,
        grid=(2, nc),
        in_specs=[
            pl.BlockSpec((L, M_QK_WIDTH), lambda d, c: (cidx(d, c), OFF_MQ // M_QK_WIDTH)),
            pl.BlockSpec((M_QK_WIDTH, L), lambda d, c: (0, cidx(d, c))),
            pl.BlockSpec((L, half_v), lambda d, c: (cidx(d, c), OFF_MV // half_v)),
            pl.BlockSpec((L, half_v), lambda d, c: (cidx(d, c), OFF_MV // half_v + 1)),
            pl.BlockSpec((N_IF, L), lambda d, c: (0, cidx(d, c))),
            pl.BlockSpec((N_IF, LANES), lambda d, c: (0, 0)),
        ],
        out_specs=pl.BlockSpec((None, L, M_WIDTH), lambda d, c: (d, cidx(d, c), 0)),
        out_shape=jax.ShapeDtypeStruct((2, s_len, M_WIDTH), BF16),
        scratch_shapes=[
            pltpu.VMEM((N_M_HEADS, M_QK_DIM, M_V_DIM + LANES), F32),
            pltpu.VMEM((N_M_HEADS, LANES), F32),
        ],
        compiler_params=_cparams(("parallel", "arbitrary")),
        name="mlstm_scan",
    )(u, kt, u, u, gates_t, bias_tab)


def _mlstm_post_kernel(h_ref, o_ref, g_ref, out_ref):
    hsum = h_ref[0].astype(F32) + h_ref[1].astype(F32)
    hn = hsum * lax.rsqrt(jnp.mean(hsum * hsum, axis=-1, keepdims=True) + NORM_EPS)
    out_ref[...] = (hn * g_ref[...] * jax.nn.sigmoid(o_ref[...].astype(F32))).astype(out_ref.dtype)


def mlstm_post(hd, u, g_mh, tm=512):
    s_len = u.shape[0]
    tm = min(tm, s_len)
    return pl.pallas_call(
        _mlstm_post_kernel,
        grid=(s_len // tm, N_M_HEADS),
        in_specs=[
            pl.BlockSpec((2, tm, M_V_DIM), lambda i, h: (0, i, h)),
            pl.BlockSpec((tm, M_V_DIM), lambda i, h: (i, OFF_MO // M_V_DIM + h)),
            pl.BlockSpec((1, M_V_DIM), lambda i, h: (0, h)),
        ],
        out_specs=pl.BlockSpec((tm, M_V_DIM), lambda i, h: (i, h)),
        out_shape=jax.ShapeDtypeStruct((s_len, M_WIDTH), BF16),
        compiler_params=_cparams(("parallel", "parallel")),
        name="mlstm_post",
    )(hd, u, g_mh.reshape(1, M_WIDTH).astype(F32))


def _merge_kernel(a_ref, m_ref, wa_ref, wm_ref, ga_ref, gm_ref, o_ref):
    ya = jnp.dot(a_ref[...], wa_ref[...], preferred_element_type=F32)
    ym = jnp.dot(m_ref[...], wm_ref[...], preferred_element_type=F32)
    ga = jax.nn.sigmoid(ga_ref[...].astype(F32))
    gm = jax.nn.sigmoid(gm_ref[...].astype(F32))
    o_ref[...] = (ga * ya + gm * ym).astype(o_ref.dtype)


def branch_merge(attn, mh, w_a, w_m, u, tm=1024, tn=1024):
    m, ka = attn.shape
    km = mh.shape[1]
    d = w_a.shape[1]
    tm = min(tm, m)
    gblk = OFF_G // tn
    return pl.pallas_call(
        _merge_kernel,
        grid=(m // tm, d // tn),
        in_specs=[
            pl.BlockSpec((tm, ka), lambda i, j: (i, 0)),
            pl.BlockSpec((tm, km), lambda i, j: (i, 0)),
            pl.BlockSpec((ka, tn), lambda i, j: (0, j)),
            pl.BlockSpec((km, tn), lambda i, j: (0, j)),
            pl.BlockSpec((tm, tn), lambda i, j: (i, gblk + j)),
            pl.BlockSpec((tm, tn), lambda i, j: (i, gblk + d // tn + j)),
        ],
        out_specs=pl.BlockSpec((tm, tn), lambda i, j: (i, j)),
        out_shape=jax.ShapeDtypeStruct((m, d), BF16),
        compiler_params=_cparams(("parallel", "parallel")),
        name="branch_merge",
    )(attn, mh, w_a, w_m, u, u)


def _resid_mm_kernel(a_ref, w_ref, x_ref, o_ref):
    o_ref[...] = x_ref[...] + jnp.dot(a_ref[...], w_ref[...], preferred_element_type=F32)


def residual_matmul(a, w, x, tm=1024, tn=1024):
    m, k = a.shape
    n = w.shape[1]
    tm = min(tm, m)
    return pl.pallas_call(
        _resid_mm_kernel,
        grid=(m // tm, n // tn),
        in_specs=[
            pl.BlockSpec((tm, k), lambda i, j: (i, 0)),
            pl.BlockSpec((k, tn), lambda i, j: (0, j)),
            pl.BlockSpec((tm, tn), lambda i, j: (i, j)),
        ],
        out_specs=pl.BlockSpec((tm, tn), lambda i, j: (i, j)),
        out_shape=jax.ShapeDtypeStruct((m, n), F32),
        input_output_aliases={2: 0},
        compiler_params=_cparams(("parallel", "parallel")),
        name="residual_matmul",
    )(a, w, x)


def _router_kernel(x_ref, g_ref, w_ref, o_ref):
    x = x_ref[...]
    ms = jnp.mean(x * x, axis=-1, keepdims=True)
    xn = (x * lax.rsqrt(ms + NORM_EPS) * g_ref[...]).astype(BF16)
    logits = jnp.dot(xn, w_ref[...], preferred_element_type=F32)
    lane = lax.broadcasted_iota(I32, logits.shape, 1)
    logits = jnp.where(lane < N_EXPERTS, logits, -jnp.inf)
    e = jnp.exp(logits - jnp.max(logits, axis=-1, keepdims=True))
    o_ref[...] = e / jnp.sum(e, axis=-1, keepdims=True)


def router(x, g, w_pad, tm=512):
    m, d = x.shape
    tm = min(tm, m)
    return pl.pallas_call(
        _router_kernel,
        grid=(m // tm,),
        in_specs=[
            pl.BlockSpec((tm, d), lambda i: (i, 0)),
            pl.BlockSpec((1, d), lambda i: (0, 0)),
            pl.BlockSpec((d, LANES), lambda i: (0, 0)),
        ],
        out_specs=pl.BlockSpec((tm, LANES), lambda i: (i, 0)),
        out_shape=jax.ShapeDtypeStruct((m, LANES), F32),
        compiler_params=_cparams(("parallel",)),
        name="router",
    )(x, g.reshape(1, d).astype(F32), w_pad)


def _select_kernel(aff_ref, idx_ref, val_ref, *, cap):
    a = aff_ref[0]
    nb = a.shape[0]
    bits = pltpu.bitcast(a, I32)
    capf = jnp.float32(cap)

    def count(mask):
        return jnp.sum(jnp.sum(mask.astype(F32), axis=1, keepdims=True), axis=0, keepdims=True)

    def body(i, t):
        cand = t | jnp.left_shift(jnp.int32(1), jnp.int32(30) - i)
        return jnp.where(count(bits >= cand) >= capf, cand, t)

    thr = lax.fori_loop(0, 31, body, jnp.zeros((1, 1), I32))
    gt = bits > thr
    eq = bits == thr
    need = capf - count(gt)

    k_i = lax.broadcasted_iota(I32, (LANES, LANES), 0)
    s_i = lax.broadcasted_iota(I32, (LANES, LANES), 1)
    incl = (k_i <= s_i).astype(BF16)
    rb = lax.broadcasted_iota(I32, (nb, nb), 0)
    cb = lax.broadcasted_iota(I32, (nb, nb), 1)
    blk_strict = (cb < rb).astype(BF16)
    blk_incl = (rb <= cb).astype(BF16)

    eq_b = eq.astype(BF16)
    eq_in = jnp.dot(eq_b, incl, preferred_element_type=F32)
    eq_tot = jnp.broadcast_to(eq_in[:, LANES - 1:LANES], (nb, LANES)).astype(BF16)
    eq_off = jnp.dot(blk_strict, eq_tot, preferred_element_type=F32)
    eq_before = eq_off + eq_in - eq.astype(F32)
    sel = gt | (eq & (eq_before < need))

    sel_b = sel.astype(BF16)
    cin = jnp.dot(sel_b, incl, preferred_element_type=F32)
    tot_row = lax.dot_general(jnp.ones((8, LANES), BF16), sel_b, (((1,), (1,)), ((), ())),
                              preferred_element_type=F32)
    binc_row = jnp.dot(tot_row.astype(BF16), blk_incl, preferred_element_type=F32)[0:1, :]
    bexc_row = binc_row - tot_row[0:1, :]

    j_col = lax.broadcasted_iota(I32, (cap, 1), 0).astype(F32)
    kj = jnp.sum((binc_row <= j_col).astype(F32), axis=1, keepdims=True)
    blk_lane = lax.broadcasted_iota(I32, (cap, nb), 1).astype(F32)
    onehot = blk_lane == kj
    rows_c = jnp.dot(onehot.astype(BF16), cin.astype(BF16), preferred_element_type=F32)
    bex_k = jnp.sum(jnp.where(onehot, bexc_row, 0.0), axis=1, keepdims=True)
    r = j_col - bex_k
    pos = jnp.sum((rows_c <= r).astype(F32), axis=1, keepdims=True)
    idx_ref[0] = (kj * LANES + pos).astype(I32)
    rows_a = jnp.dot(onehot.astype(F32), a, preferred_element_type=F32, precision=lax.Precision.HIGHEST)
    pos_lane = lax.broadcasted_iota(I32, (cap, LANES), 1).astype(F32)
    val_ref[0] = jnp.sum(jnp.where(pos_lane == pos, rows_a, 0.0), axis=1, keepdims=True)


def expert_select(aff3, cap):
    ne, nb, _ = aff3.shape
    return pl.pallas_call(
        functools.partial(_select_kernel, cap=cap),
        grid=(ne,),
        in_specs=[pl.BlockSpec((1, nb, LANES), lambda e: (e, 0, 0))],
        out_specs=[pl.BlockSpec((1, cap, 1), lambda e: (e, 0, 0)), pl.BlockSpec((1, cap, 1), lambda e: (e, 0, 0))],
        out_shape=[jax.ShapeDtypeStruct((ne, cap, 1), I32), jax.ShapeDtypeStruct((ne, cap, 1), F32)],
        compiler_params=_cparams(("parallel",)),
        name="expert_select",
    )(aff3)


def _row_copy(idx_ref, slot, r, hbm_ref, vmem_ref, sem, to_vmem):
    hbm = hbm_ref.at[pl.ds(idx_ref[slot], 1), :]
    vm = vmem_ref.at[pl.ds(r, 1), :]
    return pltpu.make_async_copy(hbm, vm, sem) if to_vmem else pltpu.make_async_copy(vm, hbm, sem)


ROW_DMA_UNROLL = 8


def _start_rows(idx_ref, base, rows, hbm_ref, vmem_ref, sem, to_vmem):
    def body(i, carry):
        r0 = pl.multiple_of(i * ROW_DMA_UNROLL, ROW_DMA_UNROLL)
        for j in range(ROW_DMA_UNROLL):
            _row_copy(idx_ref, base + r0 + j, r0 + j, hbm_ref, vmem_ref, sem, to_vmem).start()
        return carry

    lax.fori_loop(0, rows // ROW_DMA_UNROLL, body, 0)


def _wait_rows(idx_ref, base, rows, hbm_ref, vmem_ref, sem, to_vmem):
    def body(r, carry):
        _row_copy(idx_ref, base + r, r, hbm_ref, vmem_ref, sem, to_vmem).wait()
        return carry

    lax.fori_loop(0, rows, body, 0, unroll=ROW_DMA_UNROLL)


def _gather_norm_kernel(idx_ref, x_ref, g_ref, o_ref, buf_ref, sem):
    i = pl.program_id(0)
    rows = buf_ref.shape[1]
    slot = i % 2

    @pl.when(i == 0)
    def _():
        _start_rows(idx_ref, 0, rows, x_ref, buf_ref.at[0], sem.at[0], True)

    @pl.when(i + 1 < pl.num_programs(0))
    def _():
        _start_rows(idx_ref, (i + 1) * rows, rows, x_ref, buf_ref.at[1 - slot], sem.at[1 - slot], True)

    _wait_rows(idx_ref, i * rows, rows, x_ref, buf_ref.at[slot], sem.at[slot], True)
    x = buf_ref[slot]
    ms = jnp.mean(x * x, axis=-1, keepdims=True)
    o_ref[...] = (x * lax.rsqrt(ms + NORM_EPS) * g_ref[...]).astype(o_ref.dtype)


def gather_norm(idx_flat, x, g, rows=512):
    n_slots = idx_flat.shape[0]
    d = x.shape[1]
    rows = min(rows, n_slots)
    grid_spec = pltpu.PrefetchScalarGridSpec(
        num_scalar_prefetch=1,
        grid=(n_slots // rows,),
        in_specs=[pl.BlockSpec(memory_space=pl.ANY), pl.BlockSpec((1, d), lambda i, idx: (0, 0))],
        out_specs=pl.BlockSpec((rows, d), lambda i, idx: (i, 0)),
        scratch_shapes=[pltpu.VMEM((2, rows, d), F32), pltpu.SemaphoreType.DMA((2,))],
    )
    return pl.pallas_call(
        _gather_norm_kernel,
        grid_spec=grid_spec,
        out_shape=jax.ShapeDtypeStruct((n_slots, d), BF16),
        compiler_params=_cparams(("arbitrary",)),
        name="gather_norm",
    )(idx_flat, x, g.reshape(1, d).astype(F32))


FFN_GATHER_STEP = 1


def _ffn_kernel(idx_ref, x_in_ref, xs_ref, val_ref, wg_ref, wu_ref, wd_ref, x_ref, acc_ref, xbuf_ref, sems, *, cap, tn):
    del x_in_ref
    e, r, f = pl.program_id(0), pl.program_id(1), pl.program_id(2)
    n_f = pl.num_programs(2)
    rows = acc_ref.shape[0]
    base = e * cap + r * rows
    tile = e * pl.num_programs(1) + r
    g_sem, s_sem = sems.at[0], sems.at[1]

    @pl.when(f == FFN_GATHER_STEP)
    def _():
        @pl.when(tile > 0)
        def _():
            _wait_rows(idx_ref, base - rows, rows, x_ref, xbuf_ref, s_sem, False)
        _start_rows(idx_ref, base, rows, x_ref, xbuf_ref, g_sem, True)

    xs = xs_ref[...]
    hg = jnp.dot(xs, wg_ref[0], preferred_element_type=F32)
    hu = jnp.dot(xs, wu_ref[0], preferred_element_type=F32)
    hh = (hg * jax.nn.sigmoid(hg) * hu * val_ref[0]).astype(BF16)

    @pl.when(f == 0)
    def _():
        acc_ref[...] = jnp.zeros(acc_ref.shape, F32)

    for n0 in range(0, acc_ref.shape[1], tn):
        acc_ref[:, n0:n0 + tn] += jnp.dot(hh, wd_ref[0, :, n0:n0 + tn], preferred_element_type=F32)

    @pl.when(f == n_f - 1)
    def _():
        _wait_rows(idx_ref, base, rows, x_ref, xbuf_ref, g_sem, True)
        xbuf_ref[...] += acc_ref[...]
        _start_rows(idx_ref, base, rows, x_ref, xbuf_ref, s_sem, False)

        @pl.when(tile == pl.num_programs(0) * pl.num_programs(1) - 1)
        def _():
            _wait_rows(idx_ref, base, rows, x_ref, xbuf_ref, s_sem, False)


def expert_ffn(idx_flat, vals, xs, x, w_gate, w_up, w_down, cap, rows=512, tf=512, tn=512):
    n_tok, d = x.shape
    ne, _, ff = w_gate.shape
    rows = min(rows, cap)
    rt = cap // rows
    assert FFN_GATHER_STEP < ff // tf - 1
    grid_spec = pltpu.PrefetchScalarGridSpec(
        num_scalar_prefetch=1,
        grid=(ne, rt, ff // tf),
        in_specs=[
            pl.BlockSpec(memory_space=pl.ANY),
            pl.BlockSpec((rows, d), lambda e, r, f, idx: (e * rt + r, 0)),
            pl.BlockSpec((1, rows, 1), lambda e, r, f, idx: (e, r, 0)),
            pl.BlockSpec((1, d, tf), lambda e, r, f, idx: (e, 0, f)),
            pl.BlockSpec((1, d, tf), lambda e, r, f, idx: (e, 0, f)),
            pl.BlockSpec((1, tf, d), lambda e, r, f, idx: (e, f, 0)),
        ],
        out_specs=pl.BlockSpec(memory_space=pl.ANY),
        scratch_shapes=[pltpu.VMEM((rows, d), F32), pltpu.VMEM((rows, d), F32), pltpu.SemaphoreType.DMA((2,))],
    )
    return pl.pallas_call(
        functools.partial(_ffn_kernel, cap=cap, tn=tn),
        grid_spec=grid_spec,
        out_shape=jax.ShapeDtypeStruct((n_tok, d), F32),
        input_output_aliases={1: 0},
        compiler_params=_cparams(("arbitrary", "arbitrary", "arbitrary")),
        name="expert_ffn",
    )(idx_flat, x, xs, vals, w_gate, w_up, w_down)


def _rope_tables(seq_len):
    axis_dim = HEAD_DIM // 2
    rows = seq_len // GRID_W
    row_idx = jnp.repeat(jnp.arange(rows, dtype=F32), GRID_W)
    col_idx = jnp.tile(jnp.arange(GRID_W, dtype=F32), rows)
    inv_freq = ROPE_THETA ** (-jnp.arange(0, axis_dim, 2, dtype=F32) / axis_dim)
    ang_r = row_idx[:, None] * inv_freq[None, :]
    ang_c = col_idx[:, None] * inv_freq[None, :]
    cos = jnp.concatenate([jnp.cos(ang_r), jnp.cos(ang_r), jnp.cos(ang_c), jnp.cos(ang_c)], axis=-1)
    sin = jnp.concatenate([-jnp.sin(ang_r), jnp.sin(ang_r), -jnp.sin(ang_c), jnp.sin(ang_c)], axis=-1)
    return cos, sin


def _prepare_weights(p):
    w_in = p["w_in"]
    w = {
        "w_main": jnp.concatenate([w_in[:, :, :OFF_IF], w_in[:, :, OFF_IF + N_IF:]], axis=-1).astype(BF16),
        "w_if": jnp.pad(w_in[:, :, OFF_IF:OFF_IF + N_IF], ((0, 0), (0, 0), (0, LANES - N_IF))).astype(BF16),
        "w_a": p["w_br_attn"].astype(BF16),
        "w_m": p["w_br_mlstm"].astype(BF16),
        "w_out": p["w_out"].astype(BF16),
        "w_r": jnp.pad(p["w_router"], ((0, 0), (0, 0), (0, LANES - N_EXPERTS))).astype(BF16),
        "w_eg": p["w_e_gate"].astype(BF16),
        "w_eu": p["w_e_up"].astype(BF16),
        "w_ed": p["w_e_down"].astype(BF16),
    }
    gq = p["g_q"].astype(F32) * (HEAD_DIM ** -0.5 * np.log2(np.e))
    gk = p["g_k"].astype(F32)
    w["g_qk"] = jnp.concatenate([jnp.repeat(gq[:, None, :], N_Q_HEADS, axis=1),
                                 jnp.repeat(gk[:, None, :], N_KV_HEADS, axis=1)], axis=1)
    bias = jnp.concatenate([p["b_igate"].reshape(-1, 2 * N_M_HEADS), p["b_fgate"].reshape(-1, 2 * N_M_HEADS)], axis=1)
    w["gate_bias"] = jnp.broadcast_to(bias.astype(F32)[:, :, None], bias.shape + (LANES,))
    return w


def _mixer(x, l, p, w, cos_tab, sin_tab):
    xn = rmsnorm(x, p["g_mix"][l], BF16)
    u = matmul(xn, w["w_main"][l], BF16)
    gates_if = matmul(xn, w["w_if"][l], F32)
    qk = qk_prep(u, w["g_qk"][l], cos_tab, sin_tab)
    attn = flash_attention(qk, u)
    hd = mlstm_scan(u, u[:, OFF_MK:OFF_MV].T, gates_if[:, :N_IF].T, w["gate_bias"][l])
    mh = mlstm_post(hd, u, p["g_mhead"][l])
    merged = branch_merge(attn, mh, w["w_a"][l], w["w_m"][l], u)
    return residual_matmul(merged, w["w_out"][l], x)


def _expert_choice_ffn(x, l, p, w):
    s_len = x.shape[0]
    cap = CAPACITY_FACTOR * s_len // N_EXPERTS
    aff = router(x, p["g_ffn"][l], w["w_r"][l])
    aff3 = aff[:, :N_EXPERTS].T.reshape(N_EXPERTS, s_len // LANES, LANES)
    idx, vals = expert_select(aff3, cap)
    idx_flat = idx.reshape(-1)
    xs = gather_norm(idx_flat, x, p["g_ffn"][l])
    x = expert_ffn(idx_flat, vals, xs, x, w["w_eg"][l], w["w_eu"][l], w["w_ed"][l], cap)
    return x


def _trunk(x, p, w):
    s_len = x.shape[0]
    cos_tab, sin_tab = _rope_tables(s_len)
    for l in range(p["g_mix"].shape[0]):
        x = _mixer(x, l, p, w, cos_tab, sin_tab)
        x = _expert_choice_ffn(x, l, p, w)
    return rmsnorm(x, p["g_final"], F32)


def kernel(x_prompt, x_sample, g_mix, w_in, g_q, g_k, b_igate, b_fgate, g_mhead, w_br_attn, w_br_mlstm, w_out,
           g_ffn, w_router, w_e_gate, w_e_up, w_e_down, g_final):
    p = dict(g_mix=g_mix, w_in=w_in, g_q=g_q, g_k=g_k, b_igate=b_igate, b_fgate=b_fgate, g_mhead=g_mhead,
             w_br_attn=w_br_attn, w_br_mlstm=w_br_mlstm, w_out=w_out, g_ffn=g_ffn, w_router=w_router,
             w_e_gate=w_e_gate, w_e_up=w_e_up, w_e_down=w_e_down, g_final=g_final)
    w = _prepare_weights(p)
    outs = []
    for x in (x_prompt, x_sample):
        b, s_len, d = x.shape
        assert b == 1, "tokens of different batch entries must not attend to each other"
        y = _trunk(x.reshape(b * s_len, d), p, w)
        outs.append(y.reshape(b, s_len, d))
    return tuple(outs)
```

```python
import functools

import jax
import jax.numpy as jnp
import numpy as np
from jax import lax
from jax.experimental import pallas as pl
from jax.experimental.pallas import tpu as pltpu

F32 = jnp.float32
BF16 = jnp.bfloat16
I32 = jnp.int32

GRID_W = 64
HEAD_DIM = 128
N_Q_HEADS = 16
N_KV_HEADS = 4
Q_PER_KV = N_Q_HEADS // N_KV_HEADS
ATTN_WIDTH = N_Q_HEADS * HEAD_DIM
KV_WIDTH = N_KV_HEADS * HEAD_DIM
ROPE_THETA = 10000.0
N_M_HEADS = 8
M_QK_DIM = 128
M_V_DIM = 256
M_QK_WIDTH = N_M_HEADS * M_QK_DIM
M_WIDTH = N_M_HEADS * M_V_DIM
N_EXPERTS = 16
CAPACITY_FACTOR = 2
NORM_EPS = 1e-6

OFF_Q = 0
OFF_K = OFF_Q + ATTN_WIDTH
OFF_V = OFF_K + KV_WIDTH
OFF_MQ = OFF_V + KV_WIDTH
OFF_MK = OFF_MQ + M_QK_WIDTH
OFF_MV = OFF_MK + M_QK_WIDTH
OFF_MO = OFF_MV + M_WIDTH
OFF_IF = OFF_MO + M_WIDTH
N_IF = 4 * N_M_HEADS
N_MAIN = 17408

LANES = 128
VMEM_LIMIT = 56 * 1024 * 1024

M_CHUNK = 128


def _cparams(sem, vmem=VMEM_LIMIT):
    return pltpu.CompilerParams(dimension_semantics=sem, vmem_limit_bytes=vmem)


def _rmsnorm_kernel(x_ref, g_ref, o_ref):
    x = x_ref[...].astype(F32)
    ms = jnp.mean(x * x, axis=-1, keepdims=True)
    o_ref[...] = (x * lax.rsqrt(ms + NORM_EPS) * g_ref[...]).astype(o_ref.dtype)


def rmsnorm(x, g, out_dtype, tm=512):
    m, d = x.shape
    return pl.pallas_call(
        _rmsnorm_kernel,
        grid=(m // tm,),
        in_specs=[pl.BlockSpec((tm, d), lambda i: (i, 0)), pl.BlockSpec((1, d), lambda i: (0, 0))],
        out_specs=pl.BlockSpec((tm, d), lambda i: (i, 0)),
        out_shape=jax.ShapeDtypeStruct((m, d), out_dtype),
        compiler_params=_cparams(("parallel",)),
        name="rmsnorm",
    )(x, g.reshape(1, d).astype(F32))


def _mm_kernel(a_ref, b_ref, o_ref):
    o_ref[...] = jnp.dot(a_ref[...], b_ref[...], preferred_element_type=F32).astype(o_ref.dtype)


def matmul(a, b, out_dtype, tm=1024, tn=1024, n=None):
    m, k = a.shape
    n = b.shape[1] if n is None else n
    tm, tn = min(tm, m), min(tn, n)
    return pl.pallas_call(
        _mm_kernel,
        grid=(m // tm, n // tn),
        in_specs=[pl.BlockSpec((tm, k), lambda i, j: (i, 0)), pl.BlockSpec((k, tn), lambda i, j: (0, j))],
        out_specs=pl.BlockSpec((tm, tn), lambda i, j: (i, j)),
        out_shape=jax.ShapeDtypeStruct((m, n), out_dtype),
        compiler_params=_cparams(("parallel", "parallel")),
        name="matmul",
    )(a, b)


def _qkprep_kernel(u_ref, g_ref, c_ref, s_ref, o_ref):
    n_heads = o_ref.shape[1] // HEAD_DIM
    cos = c_ref[...]
    sin = s_ref[...]
    lane = lax.broadcasted_iota(I32, cos.shape, 1)
    first_half = (lane % (HEAD_DIM // 2)) < (HEAD_DIM // 4)
    for h in range(n_heads):
        sl = slice(h * HEAD_DIM, (h + 1) * HEAD_DIM)
        x = u_ref[:, sl].astype(F32)
        ms = jnp.mean(x * x, axis=-1, keepdims=True)
        xn = x * lax.rsqrt(ms + NORM_EPS) * g_ref[h:h + 1, :]
        partner = jnp.where(first_half, pltpu.roll(xn, HEAD_DIM - HEAD_DIM // 4, 1), pltpu.roll(xn, HEAD_DIM // 4, 1))
        o_ref[:, sl] = (xn * cos + partner * sin).astype(o_ref.dtype)


def qk_prep(u, g_tab, cos_tab, sin_tab, tm=512):
    m = u.shape[0]
    w = ATTN_WIDTH + KV_WIDTH
    return pl.pallas_call(
        _qkprep_kernel,
        grid=(m // tm,),
        in_specs=[
            pl.BlockSpec((tm, w), lambda i: (i, 0)),
            pl.BlockSpec((w // HEAD_DIM, HEAD_DIM), lambda i: (0, 0)),
            pl.BlockSpec((tm, HEAD_DIM), lambda i: (i, 0)),
            pl.BlockSpec((tm, HEAD_DIM), lambda i: (i, 0)),
        ],
        out_specs=pl.BlockSpec((tm, w), lambda i: (i, 0)),
        out_shape=jax.ShapeDtypeStruct((m, w), BF16),
        compiler_params=_cparams(("parallel",)),
        name="qk_prep",
    )(u, g_tab, cos_tab, sin_tab)


def _flash_kernel(q_ref, k_ref, v_ref, o_ref, m_ref, l_ref, acc_ref):
    ki = pl.program_id(2)
    tq = q_ref.shape[0]
    tk = k_ref.shape[0]

    @pl.when(ki == 0)
    def _():
        m_ref[...] = jnp.full(m_ref.shape, -jnp.inf, F32)
        l_ref[...] = jnp.zeros(l_ref.shape, F32)
        acc_ref[...] = jnp.zeros(acc_ref.shape, F32)

    k = k_ref[...]
    v = v_ref[...]
    for g in range(Q_PER_KV):
        rs = slice(g * tq, (g + 1) * tq)
        q = q_ref[:, g * HEAD_DIM:(g + 1) * HEAD_DIM]
        s = lax.dot_general(q, k, (((1,), (1,)), ((), ())), preferred_element_type=F32)
        m_prev = m_ref[rs, :]
        m_new = jnp.maximum(m_prev, jnp.max(s, axis=-1, keepdims=True))
        alpha = jnp.exp2(m_prev - m_new)
        p = jnp.exp2(s - jnp.tile(m_new, (1, tk // LANES)))
        l_ref[rs, :] = alpha * l_ref[rs, :] + jnp.sum(p, axis=-1, keepdims=True)
        acc_ref[rs, :] = alpha * acc_ref[rs, :] + jnp.dot(p.astype(BF16), v, preferred_element_type=F32)
        m_ref[rs, :] = m_new

    @pl.when(ki == pl.num_programs(2) - 1)
    def _():
        for g in range(Q_PER_KV):
            rs = slice(g * tq, (g + 1) * tq)
            o_ref[:, g * HEAD_DIM:(g + 1) * HEAD_DIM] = (acc_ref[rs, :] / l_ref[rs, :]).astype(o_ref.dtype)


def flash_attention(qk, u, tq=1024, tk=2048):
    s_len = qk.shape[0]
    tq, tk = min(tq, s_len), min(tk, s_len)
    gw = Q_PER_KV * HEAD_DIM
    rows = Q_PER_KV * tq
    return pl.pallas_call(
        _flash_kernel,
        grid=(N_KV_HEADS, s_len // tq, s_len // tk),
        in_specs=[
            pl.BlockSpec((tq, gw), lambda h, i, j: (i, h)),
            pl.BlockSpec((tk, HEAD_DIM), lambda h, i, j: (j, ATTN_WIDTH // HEAD_DIM + h)),
            pl.BlockSpec((tk, HEAD_DIM), lambda h, i, j: (j, OFF_V // HEAD_DIM + h)),
        ],
        out_specs=pl.BlockSpec((tq, gw), lambda h, i, j: (i, h)),
        out_shape=jax.ShapeDtypeStruct((s_len, ATTN_WIDTH), BF16),
        scratch_shapes=[
            pltpu.VMEM((rows, LANES), F32),
            pltpu.VMEM((rows, LANES), F32),
            pltpu.VMEM((rows, HEAD_DIM), F32),
        ],
        compiler_params=_cparams(("parallel", "parallel", "arbitrary")),
        name="flash_attention",
    )(qk, qk, u)


def _log_sigmoid(x):
    return jnp.minimum(x, 0.0) - jnp.log1p(jnp.exp(-jnp.abs(x)))


def _mlstm_kernel(q_ref, kt_ref, v0_ref, v1_ref, gate_ref, bias_ref, o_ref, st_ref, m_ref):
    d = pl.program_id(0)
    c = pl.program_id(1)
    L = q_ref.shape[0]
    H = N_M_HEADS
    DK, DV = M_QK_DIM, M_V_DIM

    @pl.when(c == 0)
    def _():
        st_ref[...] = jnp.zeros(st_ref.shape, F32)
        m_ref[...] = jnp.zeros(m_ref.shape, F32)

    r_i = pl.multiple_of(d * H, H)
    r_f = pl.multiple_of(2 * H + d * H, H)
    i_rows = gate_ref[pl.ds(r_i, H), :] + bias_ref[pl.ds(r_i, H), :]
    f_rows = _log_sigmoid(gate_ref[pl.ds(r_f, H), :] + bias_ref[pl.ds(r_f, H), :])

    t_idx = lax.broadcasted_iota(I32, (L, L), 0)
    s_idx = lax.broadcasted_iota(I32, (L, L), 1)
    ahead = (s_idx - t_idx) * (1 - 2 * d)
    valid = ahead <= 0
    csum = jnp.where(ahead >= 0, 1.0, 0.0).astype(F32)
    b_rows = jnp.dot(f_rows, csum, preferred_element_type=F32, precision=lax.Precision.HIGHEST)
    totals = jnp.sum(f_rows, axis=-1, keepdims=True)
    g_rows = totals - b_rows + i_rows
    m_prevs = m_ref[...]
    m_news = jnp.maximum(totals + m_prevs, jnp.max(g_rows, axis=-1, keepdims=True))
    decays = jnp.exp(totals + m_prevs - m_news)
    wk_rows = jnp.exp(g_rows - m_news[:, 0:1])
    m_ref[...] = m_news
    ones_col = (lax.broadcasted_iota(I32, (L, LANES), 1) == 0).astype(BF16)

    for h in range(H):
        b_row = b_rows[h:h + 1, :]
        i_row = i_rows[h:h + 1, :]
        b_col = jnp.transpose(jnp.broadcast_to(b_row, (L, L)))

        m_prev = m_prevs[h:h + 1, 0:1]
        dmat = jnp.where(valid, b_col - b_row + i_row, -jnp.inf)
        inter = b_col[:, 0:1] + m_prev
        m_t = jnp.maximum(inter, jnp.max(dmat, axis=-1, keepdims=True))
        dexp = jnp.exp(dmat - m_t)

        qb = q_ref[:, h * DK:(h + 1) * DK]
        kts = (kt_ref[h * DK:(h + 1) * DK, :].astype(F32) * (DK ** -0.5))
        v_ref = v0_ref if h < H // 2 else v1_ref
        hv = h % (H // 2)
        v_ext = jnp.concatenate([v_ref[:, hv * DV:(hv + 1) * DV], ones_col], axis=1)
        qk = jnp.dot(qb, kts.astype(BF16), preferred_element_type=F32)
        w = dexp * qk
        a_int = jnp.exp(inter - m_t)
        st = st_ref[h]
        q_st = jnp.dot(qb, st.astype(BF16), preferred_element_type=F32)
        num = jnp.dot(w.astype(BF16), v_ext[:, :DV], preferred_element_type=F32) + a_int * q_st[:, :DV]
        den = jnp.sum(w, axis=-1, keepdims=True) + a_int * q_st[:, DV:DV + 1]
        o_ref[:, h * DV:(h + 1) * DV] = (num / jnp.maximum(jnp.abs(den), jnp.exp(-m_t))).astype(o_ref.dtype)

        kwt = (kts * wk_rows[h:h + 1, :]).astype(BF16)
        st_ref[h] = decays[h:h + 1, 0:1] * st + jnp.dot(kwt, v_ext, preferred_element_type=F32)


def mlstm_scan(u, kt, gates_t, bias_tab):
    s_len = u.shape[0]
    L = M_CHUNK
    nc = s_len // L
    half_v = M_WIDTH // 2

    def cidx(d, c):
        return c + d * (nc - 1 - 2 * c)

    return pl.pallas_call(
        _mlstm_kernel,
        grid=(2, nc),
        in_specs=[
            pl.BlockSpec((L, M_QK_WIDTH), lambda d, c: (cidx(d, c), OFF_MQ // M_QK_WIDTH)),
            pl.BlockSpec((M_QK_WIDTH, L), lambda d, c: (0, cidx(d, c))),
            pl.BlockSpec((L, half_v), lambda d, c: (cidx(d, c), OFF_MV // half_v)),
            pl.BlockSpec((L, half_v), lambda d, c: (cidx(d, c), OFF_MV // half_v + 1)),
            pl.BlockSpec((N_IF, L), lambda d, c: (0, cidx(d, c))),
            pl.BlockSpec((N_IF, LANES), lambda d, c: (0, 0)),
        ],
        out_specs=pl.BlockSpec((None, L, M_WIDTH), lambda d, c: (d, cidx(d, c), 0)),
        out_shape=jax.ShapeDtypeStruct((2, s_len, M_WIDTH), BF16),
        scratch_shapes=[
            pltpu.VMEM((N_M_HEADS, M_QK_DIM, M_V_DIM + LANES), F32),
            pltpu.VMEM((N_M_HEADS, LANES), F32),
        ],
        compiler_params=_cparams(("parallel", "arbitrary")),
        name="mlstm_scan",
    )(u, kt, u, u, gates_t, bias_tab)


def _mlstm_post_kernel(h_ref, o_ref, g_ref, out_ref):
    hsum = h_ref[0].astype(F32) + h_ref[1].astype(F32)
    hn = hsum * lax.rsqrt(jnp.mean(hsum * hsum, axis=-1, keepdims=True) + NORM_EPS)
    out_ref[...] = (hn * g_ref[...] * jax.nn.sigmoid(o_ref[...].astype(F32))).astype(out_ref.dtype)


def mlstm_post(hd, u, g_mh, tm=512):
    s_len = u.shape[0]
    tm = min(tm, s_len)
    return pl.pallas_call(
        _mlstm_post_kernel,
        grid=(s_len // tm, N_M_HEADS),
        in_specs=[
            pl.BlockSpec((2, tm, M_V_DIM), lambda i, h: (0, i, h)),
            pl.BlockSpec((tm, M_V_DIM), lambda i, h: (i, OFF_MO // M_V_DIM + h)),
            pl.BlockSpec((1, M_V_DIM), lambda i, h: (0, h)),
        ],
        out_specs=pl.BlockSpec((tm, M_V_DIM), lambda i, h: (i, h)),
        out_shape=jax.ShapeDtypeStruct((s_len, M_WIDTH), BF16),
        compiler_params=_cparams(("parallel", "parallel")),
        name="mlstm_post",
    )(hd, u, g_mh.reshape(1, M_WIDTH).astype(F32))


def _shifted_gate(lo_ref, hi):
    g = jnp.concatenate([lo_ref[...].astype(F32), hi.astype(F32)], axis=1)
    return g[:, N_IF:N_IF + lo_ref.shape[1]]


def _merge_kernel(a_ref, m_ref, wa_ref, wm_ref, ga_ref, ga_hi_ref, gm_ref, gm_hi_ref, tail_ref, o_ref):
    ya = jnp.dot(a_ref[...], wa_ref[...], preferred_element_type=F32)
    ym = jnp.dot(m_ref[...], wm_ref[...], preferred_element_type=F32)
    last = pl.program_id(1) == pl.num_programs(1) - 1
    gm_hi = jnp.where(last, tail_ref[...].astype(F32), gm_hi_ref[...].astype(F32))
    ga = jax.nn.sigmoid(_shifted_gate(ga_ref, ga_hi_ref[...]))
    gm = jax.nn.sigmoid(_shifted_gate(gm_ref, gm_hi))
    o_ref[...] = (ga * ya + gm * ym).astype(o_ref.dtype)


def branch_merge(attn, mh, w_a, w_m, u, u_tail, tail_blk, tm=1024, tn=1024):
    m, ka = attn.shape
    km = mh.shape[1]
    d = w_a.shape[1]
    tm = min(tm, m)
    gblk = OFF_IF // tn
    per = tn // LANES
    last_hi = u.shape[1] // LANES - 1
    return pl.pallas_call(
        _merge_kernel,
        grid=(m // tm, d // tn),
        in_specs=[
            pl.BlockSpec((tm, ka), lambda i, j: (i, 0)),
            pl.BlockSpec((tm, km), lambda i, j: (i, 0)),
            pl.BlockSpec((ka, tn), lambda i, j: (0, j)),
            pl.BlockSpec((km, tn), lambda i, j: (0, j)),
            pl.BlockSpec((tm, tn), lambda i, j: (i, gblk + j)),
            pl.BlockSpec((tm, LANES), lambda i, j: (i, (gblk + j + 1) * per)),
            pl.BlockSpec((tm, tn), lambda i, j: (i, gblk + d // tn + j)),
            pl.BlockSpec((tm, LANES), lambda i, j: (i, jnp.minimum((gblk + d // tn + j + 1) * per, last_hi))),
            pl.BlockSpec((tm, LANES), lambda i, j: (i, tail_blk)),
        ],
        out_specs=pl.BlockSpec((tm, tn), lambda i, j: (i, j)),
        out_shape=jax.ShapeDtypeStruct((m, d), BF16),
        compiler_params=_cparams(("parallel", "parallel")),
        name="branch_merge",
    )(attn, mh, w_a, w_m, u, u, u, u, u_tail)


def _resid_mm_kernel(a_ref, w_ref, x_ref, o_ref):
    o_ref[...] = x_ref[...] + jnp.dot(a_ref[...], w_ref[...], preferred_element_type=F32)


def residual_matmul(a, w, x, tm=1024, tn=1024):
    m, k = a.shape
    n = w.shape[1]
    tm = min(tm, m)
    return pl.pallas_call(
        _resid_mm_kernel,
        grid=(m // tm, n // tn),
        in_specs=[
            pl.BlockSpec((tm, k), lambda i, j: (i, 0)),
            pl.BlockSpec((k, tn), lambda i, j: (0, j)),
            pl.BlockSpec((tm, tn), lambda i, j: (i, j)),
        ],
        out_specs=pl.BlockSpec((tm, tn), lambda i, j: (i, j)),
        out_shape=jax.ShapeDtypeStruct((m, n), F32),
        input_output_aliases={2: 0},
        compiler_params=_cparams(("parallel", "parallel")),
        name="residual_matmul",
    )(a, w, x)


def _router_kernel(x_ref, g_ref, w_ref, o_ref):
    x = x_ref[...]
    ms = jnp.mean(x * x, axis=-1, keepdims=True)
    xn = (x * lax.rsqrt(ms + NORM_EPS) * g_ref[...]).astype(BF16)
    logits = jnp.dot(xn, w_ref[...], preferred_element_type=F32)
    lane = lax.broadcasted_iota(I32, logits.shape, 1)
    logits = jnp.where(lane < N_EXPERTS, logits, -jnp.inf)
    e = jnp.exp(logits - jnp.max(logits, axis=-1, keepdims=True))
    o_ref[...] = e / jnp.sum(e, axis=-1, keepdims=True)


def router(x, g, w_pad, tm=512):
    m, d = x.shape
    tm = min(tm, m)
    return pl.pallas_call(
        _router_kernel,
        grid=(m // tm,),
        in_specs=[
            pl.BlockSpec((tm, d), lambda i: (i, 0)),
            pl.BlockSpec((1, d), lambda i: (0, 0)),
            pl.BlockSpec((d, LANES), lambda i: (0, 0)),
        ],
        out_specs=pl.BlockSpec((tm, LANES), lambda i: (i, 0)),
        out_shape=jax.ShapeDtypeStruct((m, LANES), F32),
        compiler_params=_cparams(("parallel",)),
        name="router",
    )(x, g.reshape(1, d).astype(F32), w_pad)


def _select_kernel(aff_ref, idx_ref, val_ref, *, cap):
    a = aff_ref[0]
    nb = a.shape[0]
    bits = pltpu.bitcast(a, I32)
    capf = jnp.float32(cap)

    def count(mask):
        return jnp.sum(jnp.sum(mask.astype(F32), axis=1, keepdims=True), axis=0, keepdims=True)

    def body(i, t):
        cand = t | jnp.left_shift(jnp.int32(1), jnp.int32(30) - i)
        return jnp.where(count(bits >= cand) >= capf, cand, t)

    thr = lax.fori_loop(0, 31, body, jnp.zeros((1, 1), I32))
    gt = bits > thr
    eq = bits == thr
    need = capf - count(gt)

    k_i = lax.broadcasted_iota(I32, (LANES, LANES), 0)
    s_i = lax.broadcasted_iota(I32, (LANES, LANES), 1)
    incl = (k_i <= s_i).astype(BF16)
    rb = lax.broadcasted_iota(I32, (nb, nb), 0)
    cb = lax.broadcasted_iota(I32, (nb, nb), 1)
    blk_strict = (cb < rb).astype(BF16)
    blk_incl = (rb <= cb).astype(BF16)

    eq_b = eq.astype(BF16)
    eq_in = jnp.dot(eq_b, incl, preferred_element_type=F32)
    eq_tot = jnp.broadcast_to(eq_in[:, LANES - 1:LANES], (nb, LANES)).astype(BF16)
    eq_off = jnp.dot(blk_strict, eq_tot, preferred_element_type=F32)
    eq_before = eq_off + eq_in - eq.astype(F32)
    sel = gt | (eq & (eq_before < need))

    sel_b = sel.astype(BF16)
    cin = jnp.dot(sel_b, incl, preferred_element_type=F32)
    tot_row = lax.dot_general(jnp.ones((8, LANES), BF16), sel_b, (((1,), (1,)), ((), ())),
                              preferred_element_type=F32)
    binc_row = jnp.dot(tot_row.astype(BF16), blk_incl, preferred_element_type=F32)[0:1, :]
    bexc_row = binc_row - tot_row[0:1, :]

    j_col = lax.broadcasted_iota(I32, (cap, 1), 0).astype(F32)
    kj = jnp.sum((binc_row <= j_col).astype(F32), axis=1, keepdims=True)
    blk_lane = lax.broadcasted_iota(I32, (cap, nb), 1).astype(F32)
    onehot = blk_lane == kj
    rows_c = jnp.dot(onehot.astype(BF16), cin.astype(BF16), preferred_element_type=F32)
    bex_k = jnp.sum(jnp.where(onehot, bexc_row, 0.0), axis=1, keepdims=True)
    r = j_col - bex_k
    pos = jnp.sum((rows_c <= r).astype(F32), axis=1, keepdims=True)
    idx_ref[0] = (kj * LANES + pos).astype(I32)
    rows_a = jnp.dot(onehot.astype(F32), a, preferred_element_type=F32, precision=lax.Precision.HIGHEST)
    pos_lane = lax.broadcasted_iota(I32, (cap, LANES), 1).astype(F32)
    val_ref[0] = jnp.sum(jnp.where(pos_lane == pos, rows_a, 0.0), axis=1, keepdims=True)


def expert_select(aff3, cap):
    ne, nb, _ = aff3.shape
    return pl.pallas_call(
        functools.partial(_select_kernel, cap=cap),
        grid=(ne,),
        in_specs=[pl.BlockSpec((1, nb, LANES), lambda e: (e, 0, 0))],
        out_specs=[pl.BlockSpec((1, cap, 1), lambda e: (e, 0, 0)), pl.BlockSpec((1, cap, 1), lambda e: (e, 0, 0))],
        out_shape=[jax.ShapeDtypeStruct((ne, cap, 1), I32), jax.ShapeDtypeStruct((ne, cap, 1), F32)],
        compiler_params=_cparams(("parallel",)),
        name="expert_select",
    )(aff3)


def _row_copy(idx_ref, slot, r, hbm_ref, vmem_ref, sem, to_vmem):
    hbm = hbm_ref.at[pl.ds(idx_ref[slot], 1), :]
    vm = vmem_ref.at[pl.ds(r, 1), :]
    return pltpu.make_async_copy(hbm, vm, sem) if to_vmem else pltpu.make_async_copy(vm, hbm, sem)


ROW_DMA_UNROLL = 8


def _start_rows(idx_ref, base, rows, hbm_ref, vmem_ref, sem, to_vmem):
    def body(i, carry):
        r0 = pl.multiple_of(i * ROW_DMA_UNROLL, ROW_DMA_UNROLL)
        for j in range(ROW_DMA_UNROLL):
            _row_copy(idx_ref, base + r0 + j, r0 + j, hbm_ref, vmem_ref, sem, to_vmem).start()
        return carry

    lax.fori_loop(0, rows // ROW_DMA_UNROLL, body, 0)


def _wait_rows(idx_ref, base, rows, hbm_ref, vmem_ref, sem, to_vmem):
    def body(r, carry):
        _row_copy(idx_ref, base + r, r, hbm_ref, vmem_ref, sem, to_vmem).wait()
        return carry

    lax.fori_loop(0, rows, body, 0, unroll=ROW_DMA_UNROLL)


def _gather_norm_kernel(idx_ref, x_ref, g_ref, o_ref, buf_ref, sem):
    i = pl.program_id(0)
    rows = buf_ref.shape[1]
    slot = i % 2

    @pl.when(i == 0)
    def _():
        _start_rows(idx_ref, 0, rows, x_ref, buf_ref.at[0], sem.at[0], True)

    @pl.when(i + 1 < pl.num_programs(0))
    def _():
        _start_rows(idx_ref, (i + 1) * rows, rows, x_ref, buf_ref.at[1 - slot], sem.at[1 - slot], True)

    _wait_rows(idx_ref, i * rows, rows, x_ref, buf_ref.at[slot], sem.at[slot], True)
    x = buf_ref[slot]
    ms = jnp.mean(x * x, axis=-1, keepdims=True)
    o_ref[...] = (x * lax.rsqrt(ms + NORM_EPS) * g_ref[...]).astype(o_ref.dtype)


def gather_norm(idx_flat, x, g, rows=512):
    n_slots = idx_flat.shape[0]
    d = x.shape[1]
    rows = min(rows, n_slots)
    grid_spec = pltpu.PrefetchScalarGridSpec(
        num_scalar_prefetch=1,
        grid=(n_slots // rows,),
        in_specs=[pl.BlockSpec(memory_space=pl.ANY), pl.BlockSpec((1, d), lambda i, idx: (0, 0))],
        out_specs=pl.BlockSpec((rows, d), lambda i, idx: (i, 0)),
        scratch_shapes=[pltpu.VMEM((2, rows, d), F32), pltpu.SemaphoreType.DMA((2,))],
    )
    return pl.pallas_call(
        _gather_norm_kernel,
        grid_spec=grid_spec,
        out_shape=jax.ShapeDtypeStruct((n_slots, d), BF16),
        compiler_params=_cparams(("arbitrary",)),
        name="gather_norm",
    )(idx_flat, x, g.reshape(1, d).astype(F32))


FFN_GATHER_STEP = 1


def _ffn_kernel(idx_ref, x_in_ref, xs_ref, val_ref, wg_ref, wu_ref, wd_ref, x_ref, acc_ref, xbuf_ref, sems, *, cap, tn):
    del x_in_ref
    e, r, f = pl.program_id(0), pl.program_id(1), pl.program_id(2)
    n_f = pl.num_programs(2)
    rows = acc_ref.shape[0]
    base = e * cap + r * rows
    tile = e * pl.num_programs(1) + r
    g_sem, s_sem = sems.at[0], sems.at[1]

    @pl.when(f == FFN_GATHER_STEP)
    def _():
        @pl.when(tile > 0)
        def _():
            _wait_rows(idx_ref, base - rows, rows, x_ref, xbuf_ref, s_sem, False)
        _start_rows(idx_ref, base, rows, x_ref, xbuf_ref, g_sem, True)

    xs = xs_ref[...]
    hg = jnp.dot(xs, wg_ref[0], preferred_element_type=F32)
    hu = jnp.dot(xs, wu_ref[0], preferred_element_type=F32)
    hh = (hg * jax.nn.sigmoid(hg) * hu * val_ref[0]).astype(BF16)

    @pl.when(f == 0)
    def _():
        acc_ref[...] = jnp.zeros(acc_ref.shape, F32)

    for n0 in range(0, acc_ref.shape[1], tn):
        acc_ref[:, n0:n0 + tn] += jnp.dot(hh, wd_ref[0, :, n0:n0 + tn], preferred_element_type=F32)

    @pl.when(f == n_f - 1)
    def _():
        _wait_rows(idx_ref, base, rows, x_ref, xbuf_ref, g_sem, True)
        xbuf_ref[...] += acc_ref[...]
        _start_rows(idx_ref, base, rows, x_ref, xbuf_ref, s_sem, False)

        @pl.when(tile == pl.num_programs(0) * pl.num_programs(1) - 1)
        def _():
            _wait_rows(idx_ref, base, rows, x_ref, xbuf_ref, s_sem, False)


def expert_ffn(idx_flat, vals, xs, x, w_gate, w_up, w_down, cap, rows=512, tf=512, tn=512):
    n_tok, d = x.shape
    ne, _, ff = w_gate.shape
    rows = min(rows, cap)
    rt = cap // rows
    assert FFN_GATHER_STEP < ff // tf - 1
    grid_spec = pltpu.PrefetchScalarGridSpec(
        num_scalar_prefetch=1,
        grid=(ne, rt, ff // tf),
        in_specs=[
            pl.BlockSpec(memory_space=pl.ANY),
            pl.BlockSpec((rows, d), lambda e, r, f, idx: (e * rt + r, 0)),
            pl.BlockSpec((1, rows, 1), lambda e, r, f, idx: (e, r, 0)),
            pl.BlockSpec((1, d, tf), lambda e, r, f, idx: (e, 0, f)),
            pl.BlockSpec((1, d, tf), lambda e, r, f, idx: (e, 0, f)),
            pl.BlockSpec((1, tf, d), lambda e, r, f, idx: (e, f, 0)),
        ],
        out_specs=pl.BlockSpec(memory_space=pl.ANY),
        scratch_shapes=[pltpu.VMEM((rows, d), F32), pltpu.VMEM((rows, d), F32), pltpu.SemaphoreType.DMA((2,))],
    )
    return pl.pallas_call(
        functools.partial(_ffn_kernel, cap=cap, tn=tn),
        grid_spec=grid_spec,
        out_shape=jax.ShapeDtypeStruct((n_tok, d), F32),
        input_output_aliases={1: 0},
        compiler_params=_cparams(("arbitrary", "arbitrary", "arbitrary")),
        name="expert_ffn",
    )(idx_flat, x, xs, vals, w_gate, w_up, w_down)


def _rope_tables(seq_len):
    axis_dim = HEAD_DIM // 2
    rows = seq_len // GRID_W
    row_idx = jnp.repeat(jnp.arange(rows, dtype=F32), GRID_W)
    col_idx = jnp.tile(jnp.arange(GRID_W, dtype=F32), rows)
    inv_freq = ROPE_THETA ** (-jnp.arange(0, axis_dim, 2, dtype=F32) / axis_dim)
    ang_r = row_idx[:, None] * inv_freq[None, :]
    ang_c = col_idx[:, None] * inv_freq[None, :]
    cos = jnp.concatenate([jnp.cos(ang_r), jnp.cos(ang_r), jnp.cos(ang_c), jnp.cos(ang_c)], axis=-1)
    sin = jnp.concatenate([-jnp.sin(ang_r), jnp.sin(ang_r), -jnp.sin(ang_c), jnp.sin(ang_c)], axis=-1)
    return cos, sin


def _prepare_weights(p):
    w_in = p["w_in"].astype(BF16)
    n_main = (w_in.shape[2] // LANES) * LANES
    w = {
        "w_in": w_in,
        "w_side": jnp.concatenate([
            jnp.pad(w_in[:, :, OFF_IF:OFF_IF + N_IF], ((0, 0), (0, 0), (0, LANES - N_IF))),
            jnp.pad(w_in[:, :, n_main:], ((0, 0), (0, 0), (0, LANES - (w_in.shape[2] - n_main))))], axis=-1),
        "w_a": p["w_br_attn"].astype(BF16),
        "w_m": p["w_br_mlstm"].astype(BF16),
        "w_out": p["w_out"].astype(BF16),
        "w_r": jnp.pad(p["w_router"], ((0, 0), (0, 0), (0, LANES - N_EXPERTS))).astype(BF16),
        "w_eg": p["w_e_gate"].astype(BF16),
        "w_eu": p["w_e_up"].astype(BF16),
        "w_ed": p["w_e_down"].astype(BF16),
    }
    gq = p["g_q"].astype(F32) * (HEAD_DIM ** -0.5 * np.log2(np.e))
    gk = p["g_k"].astype(F32)
    w["g_qk"] = jnp.concatenate([jnp.repeat(gq[:, None, :], N_Q_HEADS, axis=1),
                                 jnp.repeat(gk[:, None, :], N_KV_HEADS, axis=1)], axis=1)
    bias = jnp.concatenate([p["b_igate"].reshape(-1, 2 * N_M_HEADS), p["b_fgate"].reshape(-1, 2 * N_M_HEADS)], axis=1)
    w["gate_bias"] = jnp.broadcast_to(bias.astype(F32)[:, :, None], bias.shape + (LANES,))
    return w


def _mixer(x, l, p, w, cos_tab, sin_tab):
    xn = rmsnorm(x, p["g_mix"][l], BF16)
    u = matmul(xn, w["w_in"][l], BF16, n=N_MAIN)
    side = matmul(xn, w["w_side"][l], F32)
    qk = qk_prep(u, w["g_qk"][l], cos_tab, sin_tab)
    attn = flash_attention(qk, u)
    hd = mlstm_scan(u, u[:, OFF_MK:OFF_MV].T, side[:, :N_IF].T, w["gate_bias"][l])
    mh = mlstm_post(hd, u, p["g_mhead"][l])
    merged = branch_merge(attn, mh, w["w_a"][l], w["w_m"][l], u, side, 1)
    return residual_matmul(merged, w["w_out"][l], x)


def _expert_choice_ffn(x, l, p, w):
    s_len = x.shape[0]
    cap = CAPACITY_FACTOR * s_len // N_EXPERTS
    aff = router(x, p["g_ffn"][l], w["w_r"][l])
    aff3 = aff[:, :N_EXPERTS].T.reshape(N_EXPERTS, s_len // LANES, LANES)
    idx, vals = expert_select(aff3, cap)
    idx_flat = idx.reshape(-1)
    xs = gather_norm(idx_flat, x, p["g_ffn"][l])
    x = expert_ffn(idx_flat, vals, xs, x, w["w_eg"][l], w["w_eu"][l], w["w_ed"][l], cap)
    return x


def _trunk(x, p, w):
    s_len = x.shape[0]
    cos_tab, sin_tab = _rope_tables(s_len)
    for l in range(p["g_mix"].shape[0]):
        x = _mixer(x, l, p, w, cos_tab, sin_tab)
        x = _expert_choice_ffn(x, l, p, w)
    return rmsnorm(x, p["g_final"], F32)


def kernel(x_prompt, x_sample, g_mix, w_in, g_q, g_k, b_igate, b_fgate, g_mhead, w_br_attn, w_br_mlstm, w_out,
           g_ffn, w_router, w_e_gate, w_e_up, w_e_down, g_final):
    p = dict(g_mix=g_mix, w_in=w_in, g_q=g_q, g_k=g_k, b_igate=b_igate, b_fgate=b_fgate, g_mhead=g_mhead,
             w_br_attn=w_br_attn, w_br_mlstm=w_br_mlstm, w_out=w_out, g_ffn=g_ffn, w_router=w_router,
             w_e_gate=w_e_gate, w_e_up=w_e_up, w_e_down=w_e_down, g_final=g_final)
    w = _prepare_weights(p)
    outs = []
    for x in (x_prompt, x_sample):
        b, s_len, d = x.shape
        assert b == 1, "tokens of different batch entries must not attend to each other"
        y = _trunk(x.reshape(b * s_len, d), p, w)
        outs.append(y.reshape(b, s_len, d))
    return tuple(outs)
```
